```python
import jax
import jax.numpy as jnp
from jax import lax
import numpy as np

D_MODEL = 1024
BATCH = 4
SEQ = 4096
DEPTH = 2
DEC_BATCH = 128
DEC_SEQ = 8
PAST_LEN = 2048
PAGE_SIZE = 128

RMS_EPS = 1e-6
GN_EPS = 1e-5
N_BRANCHES = 3

RET_HEADS = 4
RET_DK = D_MODEL // 16
RET_DV = D_MODEL // 8
RET_QK = RET_HEADS * RET_DK
RET_V = RET_HEADS * RET_DV
RET_CHUNK = 128
ROPE_BASE = 10000.0

POOL_WINDOWS = (2, 4, 8, 16)
POOL_GROUPS = 4
POOL_WIDTH = D_MODEL // 2
POOL_GW = POOL_WIDTH // POOL_GROUPS
POOL_OUT_GW = D_MODEL // POOL_GROUPS
POOL_BUF = max(POOL_WINDOWS) - 1

ATT_PATTERNS = ((128, 1), (512, 4), (2048, 16))
N_ATT_GROUPS = 3
ATT_HEADS = 4
ATT_DH = D_MODEL // 16
ATT_W = ATT_HEADS * ATT_DH
ATT_BLOCK = 128

PEER_HEADS = 8
PEER_NKEYS = 128
PEER_EXPERTS = PEER_NKEYS * PEER_NKEYS
PEER_DKEY = 128
PEER_TOPK = 16
PEER_BLOCK = 256

IN_WIDTH = 2 * RET_QK + 2 * RET_V + POOL_WIDTH + 3 * N_ATT_GROUPS * ATT_W + N_BRANCHES * D_MODEL

kernel_name = 'hybrid_retention_pool_dilated_peer_step'


def rmsnorm(x, g):
    x32 = x.astype(jnp.float32)
    y = x32 * lax.rsqrt(jnp.mean(x32 * x32, axis=-1, keepdims=True) + RMS_EPS)
    return (y * g.astype(jnp.float32)).astype(x.dtype)


def adaln(c, w, b):
    return jnp.split(jax.nn.silu(c) @ w + b, 6, axis=-1)


def modulate(h, shift, scale):
    return h * (1 + scale[:, None, :]) + shift[:, None, :]


def split_in(z):
    sizes = (RET_QK, RET_QK, RET_V, RET_V, POOL_WIDTH,
             N_ATT_GROUPS * ATT_W, N_ATT_GROUPS * ATT_W, N_ATT_GROUPS * ATT_W)
    points = []
    acc = 0
    for s in sizes:
        acc += s
        points.append(acc)
    return jnp.split(z, points, axis=-1)


def rotary(x, pos):
    half = x.shape[-1] // 2
    inv = 1.0 / (ROPE_BASE ** jnp.linspace(0.0, 1.0, half, dtype=jnp.float32))
    ang = pos.astype(jnp.float32)[:, None] * inv[None, :]
    cos = jnp.cos(ang)[None, :, None, :]
    sin = jnp.sin(ang)[None, :, None, :]
    x1, x2 = x[..., :half], x[..., half:]
    return jnp.concatenate([x1 * cos - x2 * sin, x2 * cos + x1 * sin], axis=-1)


def ret_qkv(q, k, v, pos):
    B, T = q.shape[0], q.shape[1]
    q = rotary(q.reshape(B, T, RET_HEADS, RET_DK).astype(jnp.float32), pos)
    k = rotary(k.reshape(B, T, RET_HEADS, RET_DK).astype(jnp.float32), pos) * (RET_DK ** -0.5)
    v = v.reshape(B, T, RET_HEADS, RET_DV).astype(jnp.float32)
    return q, k, v


def ret_log_gamma():
    return jnp.log(1.0 - 2.0 ** (-5.0 - jnp.arange(RET_HEADS, dtype=jnp.float32)))


def retention_chunk(state, q, k, v):
    C = q.shape[1]
    lg = ret_log_gamma()
    idx = jnp.arange(C, dtype=jnp.float32)
    diff = idx[:, None] - idx[None, :]
    decay = jnp.where(diff >= 0, jnp.exp(lg[:, None, None] * jnp.maximum(diff, 0.0)), 0.0)
    scores = jnp.einsum('bihd,bjhd->bhij', q, k) * decay[None]
    o = jnp.einsum('bhij,bjhe->bihe', scores, v)
    cross = jnp.exp(lg[None, :] * (idx[:, None] + 1.0))
    o = o + jnp.einsum('bihd,bhde->bihe', q, state) * cross[None, :, :, None]
    kdec = k * jnp.exp(lg[None, :] * (C - 1.0 - idx)[:, None])[None, :, :, None]
    new_state = jnp.exp(lg * C)[None, :, None, None] * state + jnp.einsum('bjhd,bjhe->bhde', kdec, v)
    return o, new_state


def retention_prompt(q, k, v):
    B, S = q.shape[0], q.shape[1]
    n_chunks = S // RET_CHUNK

    def chunks(t):
        return jnp.swapaxes(t.reshape(B, n_chunks, RET_CHUNK, RET_HEADS, t.shape[-1]), 0, 1)

    def step(st, qkv):
        o, st = retention_chunk(st, *qkv)
        return st, o

    st0 = jnp.zeros((B, RET_HEADS, RET_DK, RET_DV), jnp.float32)
    st, o = lax.scan(step, st0, (chunks(q), chunks(k), chunks(v)))
    return jnp.swapaxes(o, 0, 1).reshape(B, S, RET_HEADS, RET_DV), st


def retention_out(o, g, gn_g, gn_b, w_ret_out):
    B, T = o.shape[0], o.shape[1]
    mu = jnp.mean(o, axis=-1, keepdims=True)
    var = jnp.mean(jnp.square(o - mu), axis=-1, keepdims=True)
    on = ((o - mu) * lax.rsqrt(var + GN_EPS)).reshape(B, T, RET_V)
    on = on * gn_g.astype(jnp.float32) + gn_b.astype(jnp.float32)
    y = jax.nn.silu(g.astype(jnp.float32)) * on
    return y.astype(g.dtype) @ w_ret_out


def pool_branch(ext, n_prev, pos0, pool_w, pool_scale):
    B, T = ext.shape[0], ext.shape[1] - n_prev
    e32 = ext.astype(jnp.float32)
    csp = jnp.concatenate([jnp.zeros_like(e32[:, :1]), jnp.cumsum(e32, axis=1)], axis=1)
    j = n_prev + jnp.arange(T)
    pos = pos0 + jnp.arange(T)
    groups = []
    for gi, w in enumerate(POOL_WINDOWS):
        sl = slice(gi * POOL_GW, (gi + 1) * POOL_GW)
        start = jnp.maximum(j + 1 - w, 0)
        win_sum = csp[:, j + 1, sl] - csp[:, start, sl]
        count = jnp.minimum(pos + 1, w).astype(jnp.float32)
        groups.append(win_sum / count[None, :, None] - e32[:, j, sl])
    p = jnp.stack(groups, axis=2)
    y = jnp.einsum('btgc,gcd->btgd', p, pool_w.astype(jnp.float32)).reshape(B, T, D_MODEL)
    return (y * pool_scale.astype(jnp.float32)).astype(ext.dtype)


def dilated_prompt(q, k, v, dil, n_back):
    B, S, H, E = q.shape
    L = S // dil
    Qb = ATT_BLOCK
    nb = -(-L // Qb)
    Lp = nb * Qb

    def strided(t):
        return jnp.swapaxes(t.astype(jnp.float32).reshape(B, L, dil, H, E), 1, 2)

    qb = jnp.pad(strided(q), ((0, 0), (0, 0), (0, Lp - L), (0, 0), (0, 0))).reshape(B, dil, nb, Qb, H, E)

    def key_blocks(t):
        ts = jnp.pad(strided(t), ((0, 0), (0, 0), (Qb, Lp - L), (0, 0), (0, 0))).reshape(B, dil, nb + 1, Qb, H, E)
        return jnp.concatenate([ts[:, :, :-1], ts[:, :, 1:]], axis=3)

    kb, vb = key_blocks(k), key_blocks(v)
    s = jnp.einsum('bdnihe,bdnjhe->bdnhij', qb, kb) * (E ** -0.5)
    i = jnp.arange(Qb)[:, None]
    jj = jnp.arange(2 * Qb)[None, :]
    blk = jnp.arange(nb)[:, None, None]
    dist = i + Qb - jj
    valid = (dist >= 0) & (dist <= n_back) & ((blk > 0) | (jj >= Qb))
    s = jnp.where(valid[:, None], s, -jnp.inf)
    m = jnp.max(s, axis=-1)
    p = jnp.exp(s - m[..., None])
    den = jnp.sum(p, axis=-1)
    num = jnp.einsum('bdnhij,bdnjhe->bdnihe', p, vb)

    def unstride(t):
        t = t.reshape((B, dil, Lp) + t.shape[4:])[:, :, :L]
        return jnp.swapaxes(t, 1, 2).reshape((B, S) + t.shape[3:])

    return unstride(num), unstride(jnp.swapaxes(m, 3, 4)), unstride(jnp.swapaxes(den, 3, 4))


def dilated_sample(q, k, v, buf, dil, n_back):
    B, T, H, E = q.shape
    Wb = buf.shape[1]
    ke = jnp.concatenate([buf[:, :, 0].astype(jnp.float32), k.astype(jnp.float32)], axis=1)
    ve = jnp.concatenate([buf[:, :, 1].astype(jnp.float32), v.astype(jnp.float32)], axis=1)
    idx = Wb + jnp.arange(T)[:, None] - dil * jnp.arange(n_back + 1)[None, :]
    valid = idx >= 0
    idx = jnp.maximum(idx, 0)
    kg, vg = ke[:, idx], ve[:, idx]
    s = jnp.einsum('bihe,bijhe->bhij', q.astype(jnp.float32), kg) * (E ** -0.5)
    s = jnp.where(valid[None, None], s, -jnp.inf)
    m = jnp.max(s, axis=-1)
    p = jnp.exp(s - m[..., None])
    den = jnp.sum(p, axis=-1)
    num = jnp.einsum('bhij,bijhe->bihe', p, vg)
    return num, jnp.swapaxes(m, 1, 2), jnp.swapaxes(den, 1, 2)


def merge_dilations(parts):
    num = jnp.stack([pt[0] for pt in parts])
    m = jnp.stack([pt[1] for pt in parts])
    den = jnp.stack([pt[2] for pt in parts])
    w = jnp.exp(m - jnp.max(m, axis=0, keepdims=True))
    return jnp.sum(w[..., None] * num, axis=0) / jnp.sum(w * den, axis=0)[..., None]


def merge_branches(y_r, y_p, y_a, gates, w_out):
    g_r, g_p, g_a = jnp.split(jax.nn.sigmoid(gates), N_BRANCHES, axis=-1)
    return (g_r * y_r + g_p * y_p + g_a * y_a) @ w_out


def mixer_prompt(h, w_in, ret_gn_g, ret_gn_b, w_ret_out, pool_w, pool_scale, w_att_out, w_out):
    B, S, _ = h.shape
    q_r, k_r, v_r, g_r, u_p, q_a, k_a, v_a, gates = split_in(h @ w_in)
    pos = jnp.arange(S)
    o_r, new_ret = retention_prompt(*ret_qkv(q_r, k_r, v_r, pos))
    y_r = retention_out(o_r, g_r, ret_gn_g, ret_gn_b, w_ret_out)
    y_p = pool_branch(u_p, 0, 0, pool_w, pool_scale)
    new_pool = u_p[:, S - POOL_BUF:]
    q_a, k_a, v_a = (t.reshape(B, S, N_ATT_GROUPS, ATT_HEADS, ATT_DH) for t in (q_a, k_a, v_a))
    parts, new_kv = [], []
    for gi, (win, dil) in enumerate(ATT_PATTERNS):
        parts.append(dilated_prompt(q_a[:, :, gi], k_a[:, :, gi], v_a[:, :, gi], dil, win // dil))
        keep = min(win, S)
        new_kv.append(jnp.stack([k_a[:, S - keep:, gi], v_a[:, S - keep:, gi]], axis=2))
    y_a = merge_dilations(parts).reshape(B, S, ATT_W).astype(h.dtype) @ w_att_out
    return merge_branches(y_r, y_p, y_a, gates, w_out), (new_ret, new_pool, new_kv)


def mixer_sample(h, st_ret, st_pool, bufs, w_in, ret_gn_g, ret_gn_b, w_ret_out, pool_w, pool_scale, w_att_out, w_out):
    B, T, _ = h.shape
    q_r, k_r, v_r, g_r, u_p, q_a, k_a, v_a, gates = split_in(h @ w_in)
    pos = PAST_LEN + jnp.arange(T)
    o_r, new_ret = retention_chunk(st_ret.astype(jnp.float32), *ret_qkv(q_r, k_r, v_r, pos))
    y_r = retention_out(o_r, g_r, ret_gn_g, ret_gn_b, w_ret_out)
    ext = jnp.concatenate([st_pool.astype(u_p.dtype), u_p], axis=1)
    y_p = pool_branch(ext, POOL_BUF, PAST_LEN, pool_w, pool_scale)
    new_pool = ext[:, ext.shape[1] - POOL_BUF:]
    q_a, k_a, v_a = (t.reshape(B, T, N_ATT_GROUPS, ATT_HEADS, ATT_DH) for t in (q_a, k_a, v_a))
    parts, new_kv = [], []
    for gi, (win, dil) in enumerate(ATT_PATTERNS):
        parts.append(dilated_sample(q_a[:, :, gi], k_a[:, :, gi], v_a[:, :, gi], bufs[gi], dil, win // dil))
        new_kv.append(jnp.stack([k_a[:, :, gi], v_a[:, :, gi]], axis=2))
    y_a = merge_dilations(parts).reshape(B, T, ATT_W).astype(h.dtype) @ w_att_out
    return merge_branches(y_r, y_p, y_a, gates, w_out), (new_ret, new_pool, new_kv)


def peer_ffn(h, wq, keys, u_tab, v_tab):
    B, T, D = h.shape
    x = h.reshape(B * T, D)
    n = x.shape[0]
    nblk = -(-n // PEER_BLOCK)
    xb = jnp.pad(x, ((0, nblk * PEER_BLOCK - n), (0, 0))).reshape(nblk, PEER_BLOCK, D)
    half = PEER_DKEY // 2

    def block(xt):
        q = (xt @ wq).reshape(-1, PEER_HEADS, PEER_DKEY)
        s1 = jnp.einsum('thc,hnc->thn', q[..., :half], keys[:, 0])
        s2 = jnp.einsum('thc,hnc->thn', q[..., half:], keys[:, 1])
        v1, i1 = lax.top_k(s1, PEER_TOPK)
        v2, i2 = lax.top_k(s2, PEER_TOPK)
        cand = (v1[..., :, None] + v2[..., None, :]).reshape(-1, PEER_HEADS, PEER_TOPK * PEER_TOPK)
        sc, ci = lax.top_k(cand, PEER_TOPK)
        e = (jnp.take_along_axis(i1, ci // PEER_TOPK, axis=-1) * PEER_NKEYS
             + jnp.take_along_axis(i2, ci % PEER_TOPK, axis=-1))
        g = jax.nn.softmax(sc.astype(jnp.float32), axis=-1)
        a = jax.nn.gelu(jnp.einsum('td,thkd->thk', xt, u_tab[e]).astype(jnp.float32))
        return jnp.einsum('thk,thkd->td', (g * a).astype(xt.dtype), v_tab[e])

    y = lax.map(block, xb).reshape(-1, D)[:n]
    return y.reshape(B, T, D)


def trunk_layer(x, c, mixer_fn, ada_w, ada_b, g_mix, g_ffn, peer_wq, peer_keys, peer_u, peer_v):
    sh1, sc1, ga1, sh2, sc2, ga2 = adaln(c, ada_w, ada_b)
    mix, new_states = mixer_fn(modulate(rmsnorm(x, g_mix), sh1, sc1))
    x = x + ga1[:, None, :] * mix
    ffn = peer_ffn(modulate(rmsnorm(x, g_ffn), sh2, sc2), peer_wq, peer_keys, peer_u, peer_v)
    return x + ga2[:, None, :] * ffn, new_states


def setup_inputs(seed: int = 0) -> dict:
    key = jax.random.key(seed)
    ks = jax.random.split(key, 26)

    def nrm(k, shape, scale):
        return jax.random.normal(k, shape, jnp.float32) * scale

    def kv_shape(w):
        return (DEPTH, DEC_BATCH, min(w, PAST_LEN), 2, ATT_HEADS, ATT_DH)

    return {
        'x_prompt': nrm(ks[0], (BATCH, SEQ, D_MODEL), 1.0),
        'x_sample': nrm(ks[1], (DEC_BATCH, DEC_SEQ, D_MODEL), 1.0),
        'state_ret': nrm(ks[2], (DEPTH, DEC_BATCH, RET_HEADS, RET_DK, RET_DV), 0.5),
        'state_pool': nrm(ks[3], (DEPTH, DEC_BATCH, POOL_BUF, POOL_WIDTH), 1.0),
        'cache_kv_w128': nrm(ks[4], kv_shape(ATT_PATTERNS[0][0]), 1.0),
        'cache_kv_w512': nrm(ks[5], kv_shape(ATT_PATTERNS[1][0]), 1.0),
        'cache_kv_w2048': nrm(ks[6], kv_shape(ATT_PATTERNS[2][0]), 1.0),
        'c_prompt': nrm(ks[7], (BATCH, D_MODEL), 1.0),
        'c_sample': nrm(ks[8], (DEC_BATCH, D_MODEL), 1.0),
        'ada_w': nrm(ks[9], (DEPTH, D_MODEL, 6 * D_MODEL), 0.5 * D_MODEL ** -0.5),
        'ada_b': nrm(ks[10], (DEPTH, 6 * D_MODEL), 0.01),
        'norm_mix_g': 1.0 + nrm(ks[11], (DEPTH, D_MODEL), 0.05),
        'norm_ffn_g': 1.0 + nrm(ks[12], (DEPTH, D_MODEL), 0.05),
        'w_in': nrm(ks[13], (DEPTH, D_MODEL, IN_WIDTH), D_MODEL ** -0.5),
        'ret_gn_g': 1.0 + nrm(ks[14], (DEPTH, RET_V), 0.05),
        'ret_gn_b': nrm(ks[15], (DEPTH, RET_V), 0.01),
        'w_ret_out': nrm(ks[16], (DEPTH, RET_V, D_MODEL), RET_V ** -0.5),
        'pool_w': nrm(ks[17], (DEPTH, POOL_GROUPS, POOL_GW, POOL_OUT_GW), POOL_GW ** -0.5),
        'pool_scale': 1.0 + nrm(ks[18], (DEPTH, D_MODEL), 0.1),
        'w_att_out': nrm(ks[19], (DEPTH, ATT_W, D_MODEL), ATT_W ** -0.5),
        'w_out': nrm(ks[20], (DEPTH, D_MODEL, D_MODEL), D_MODEL ** -0.5),
        'peer_wq': nrm(ks[21], (DEPTH, D_MODEL, PEER_HEADS * PEER_DKEY), D_MODEL ** -0.5),
        'peer_keys': nrm(ks[22], (DEPTH, PEER_HEADS, 2, PEER_NKEYS, PEER_DKEY // 2), (PEER_DKEY // 2) ** -0.5),
        'peer_u': nrm(ks[23], (DEPTH, PEER_EXPERTS, D_MODEL), D_MODEL ** -0.5),
        'peer_v': nrm(ks[24], (DEPTH, PEER_EXPERTS, D_MODEL), 0.5),
        'final_norm_g': 1.0 + nrm(ks[25], (D_MODEL,), 0.05),
    }


def reference(x_prompt, x_sample, state_ret, state_pool, cache_kv_w128, cache_kv_w512, cache_kv_w2048,
              c_prompt, c_sample, ada_w, ada_b, norm_mix_g, norm_ffn_g, w_in, ret_gn_g, ret_gn_b,
              w_ret_out, pool_w, pool_scale, w_att_out, w_out, peer_wq, peer_keys, peer_u, peer_v,
              final_norm_g):
    caches = (cache_kv_w128, cache_kv_w512, cache_kv_w2048)
    xp, xs = x_prompt, x_sample
    ret_p, ret_s, pool_p, pool_s = [], [], [], []
    kv_p = [[], [], []]
    kv_s = [[], [], []]
    for l in range(DEPTH):
        mw = (w_in[l], ret_gn_g[l], ret_gn_b[l], w_ret_out[l], pool_w[l], pool_scale[l], w_att_out[l], w_out[l])
        lw = (ada_w[l], ada_b[l], norm_mix_g[l], norm_ffn_g[l], peer_wq[l], peer_keys[l], peer_u[l], peer_v[l])
        bufs = (caches[0][l], caches[1][l], caches[2][l])
        xp, (st_r, st_p, kv) = trunk_layer(xp, c_prompt, lambda h: mixer_prompt(h, *mw), *lw)
        ret_p.append(st_r)
        pool_p.append(st_p)
        for gi in range(N_ATT_GROUPS):
            kv_p[gi].append(kv[gi])
        xs, (st_r, st_p, kv) = trunk_layer(
            xs, c_sample, lambda h: mixer_sample(h, state_ret[l], state_pool[l], bufs, *mw), *lw)
        ret_s.append(st_r)
        pool_s.append(st_p)
        for gi in range(N_ATT_GROUPS):
            kv_s[gi].append(kv[gi])
    y_prompt = rmsnorm(xp, final_norm_g)
    y_sample = rmsnorm(xs, final_norm_g)
    ret_state_prompt = jnp.stack(ret_p)
    ret_state_sample = jnp.stack(ret_s)
    pool_state_prompt = jnp.stack(pool_p)
    pool_state_sample = jnp.stack(pool_s)
    kv_w128_prompt = jnp.stack(kv_p[0])
    kv_w128_sample = jnp.stack(kv_s[0])
    kv_w512_prompt = jnp.stack(kv_p[1])
    kv_w512_sample = jnp.stack(kv_s[1])
    kv_w2048_prompt = jnp.stack(kv_p[2])
    kv_w2048_sample = jnp.stack(kv_s[2])
    return (y_prompt, y_sample, ret_state_prompt, ret_state_sample, pool_state_prompt, pool_state_sample,
            kv_w128_prompt, kv_w128_sample, kv_w512_prompt, kv_w512_sample, kv_w2048_prompt, kv_w2048_sample)
```

```python
import functools
import math

import jax
import jax.numpy as jnp
from jax import lax
from jax.experimental import pallas as pl
from jax.experimental.pallas import tpu as pltpu

F32 = jnp.float32
BF16 = jnp.bfloat16

D = 1024
RMS_EPS = 1e-6
GN_EPS = 1e-5
PAST_LEN = 2048

RET_HEADS = 4
RET_DK = 64
RET_DV = 128
RET_CHUNK = 128
ROPE_BASE = 10000.0
RET_LOG_GAMMA = tuple(math.log(1.0 - 2.0 ** (-5.0 - h)) for h in range(RET_HEADS))

POOL_WINDOWS = (2, 4, 8, 16)
POOL_GW = 128
POOL_HALO = 16

ATT_PATTERNS = ((128, 1), (512, 4), (2048, 16))
ATT_HEADS = 4
ATT_DH = 64
ATT_BLOCK = 128
ATT_BACK = 128
NEG = -1e30

PEER_HEADS = 8
PEER_NKEYS = 128
PEER_EXPERTS = PEER_NKEYS * PEER_NKEYS
PEER_TOPK = 16
PEER_PAIRS = tuple((a, b) for a in range(PEER_TOPK) for b in range(PEER_TOPK) if (a + 1) * (b + 1) <= PEER_TOPK)

IN_WIDTH = 7424
CB = 256
COL_GATE = 0
COL_QR, COL_KR, COL_VR, COL_GR, COL_UP = 12, 13, 14, 16, 18
COL_QA, COL_KA, COL_VA = 20, 23, 26
N_CB = IN_WIDTH // CB

VMEM_LIMIT = 56 * 1024 * 1024


def _cparams(sem):
    return pltpu.CompilerParams(dimension_semantics=sem, vmem_limit_bytes=VMEM_LIMIT)


def _mod_spec(per_row, tm, tiles_per_batch, grid_pos):
    if per_row:
        return pl.BlockSpec((1, tm, D), lambda *g: (g[grid_pos], 0, 0))
    return pl.BlockSpec((1, 1, D), lambda *g: (g[grid_pos] // tiles_per_batch, 0, 0))


def _rms_mod(x, g, sh, sc):
    ms = jnp.mean(x * x, axis=-1, keepdims=True)
    h = x * lax.rsqrt(ms + RMS_EPS) * g
    return h * (1.0 + sc) + sh


def _ada_kernel(c_ref, w_ref, b_ref, o_ref):
    c = c_ref[...]
    s = c * jax.nn.sigmoid(c)
    o_ref[0] = jnp.dot(s.astype(BF16), w_ref[0].astype(BF16), preferred_element_type=F32) + b_ref[0]


def _adaln(c_all, ada_w, ada_b):
    depth, _, n = ada_w.shape
    m = c_all.shape[0]
    tn = 512
    return pl.pallas_call(
        _ada_kernel,
        grid=(depth, n // tn),
        in_specs=[pl.BlockSpec((m, D), lambda l, j: (0, 0)),
                  pl.BlockSpec((1, D, tn), lambda l, j: (l, 0, j)),
                  pl.BlockSpec((1, 1, tn), lambda l, j: (l, 0, j))],
        out_specs=pl.BlockSpec((1, m, tn), lambda l, j: (l, 0, j)),
        out_shape=jax.ShapeDtypeStruct((depth, m, n), F32),
        compiler_params=_cparams(("arbitrary", "arbitrary")),
        name="adaln",
    )(c_all, ada_w, ada_b.reshape(depth, 1, n))


def _inproj_kernel(x_ref, g_ref, sh_ref, sc_ref, w_ref, o_ref):
    h = _rms_mod(x_ref[...], g_ref[...], sh_ref[0], sc_ref[0])
    o_ref[...] = jnp.dot(h.astype(BF16), w_ref[...], preferred_element_type=F32)


def _inproj(x, g, sh, sc, w_bf, per_row, rows_per_batch):
    m = x.shape[0]
    tm = 512
    tn = IN_WIDTH // 2
    tpb = max(rows_per_batch // tm, 1)
    return pl.pallas_call(
        _inproj_kernel,
        grid=(2, m // tm),
        in_specs=[pl.BlockSpec((tm, D), lambda j, i: (i, 0)),
                  pl.BlockSpec((1, D), lambda j, i: (0, 0)),
                  _mod_spec(per_row, tm, tpb, 1),
                  _mod_spec(per_row, tm, tpb, 1),
                  pl.BlockSpec((D, tn), lambda j, i: (0, j))],
        out_specs=pl.BlockSpec((tm, tn), lambda j, i: (i, j)),
        out_shape=jax.ShapeDtypeStruct((m, IN_WIDTH), F32),
        compiler_params=_cparams(("arbitrary", "arbitrary")),
        name="inproj",
    )(x, g, sh, sc, w_bf)


def _ret_kernel(q_ref, k_ref, v_ref, g_ref, cos_ref, sin_ref, st0_ref, gng_ref, gnb_ref,
                y_ref, st_ref, st_scr, *, C):
    c = pl.program_id(1)

    @pl.when(c == 0)
    def _():
        st_scr[...] = st0_ref[0]

    cos = cos_ref[...]
    sin = sin_ref[...]
    lane = lax.broadcasted_iota(jnp.int32, (C, RET_HEADS * RET_DK), 1)
    first_half = (lane % RET_DK) < (RET_DK // 2)

    def rot(x):
        nl = RET_HEADS * RET_DK
        swapped = jnp.where(first_half, pltpu.roll(x, nl - RET_DK // 2, 1), pltpu.roll(x, RET_DK // 2, 1))
        return x * cos + swapped * sin

    q = rot(q_ref[0])
    k = rot(k_ref[0]) * (RET_DK ** -0.5)
    v = v_ref[0]
    g = g_ref[0]
    ii = lax.broadcasted_iota(jnp.int32, (C, C), 0)
    jj = lax.broadcasted_iota(jnp.int32, (C, C), 1)
    diff = (ii - jj).astype(F32)
    causal = ii >= jj
    row = lax.broadcasted_iota(jnp.int32, (C, 1), 0).astype(F32)
    for h in range(RET_HEADS):
        lg = RET_LOG_GAMMA[h]
        qh = q[:, h * RET_DK:(h + 1) * RET_DK].astype(BF16)
        khf = k[:, h * RET_DK:(h + 1) * RET_DK]
        vh = v[:, h * RET_DV:(h + 1) * RET_DV].astype(BF16)
        st = st_scr[h]
        decay = jnp.where(causal, jnp.exp(lg * jnp.maximum(diff, 0.0)), 0.0)
        s = lax.dot_general(qh, khf.astype(BF16), (((1,), (1,)), ((), ())), preferred_element_type=F32) * decay
        o = jnp.dot(s.astype(BF16), vh, preferred_element_type=F32)
        o = o + jnp.dot(qh, st.astype(BF16), preferred_element_type=F32) * jnp.exp(lg * (row + 1.0))
        kdec = (khf * jnp.exp(lg * (C - 1.0 - row))).astype(BF16)
        st_scr[h] = math.exp(lg * C) * st + lax.dot_general(
            kdec, vh, (((0,), (0,)), ((), ())), preferred_element_type=F32)
        mu = jnp.mean(o, axis=-1, keepdims=True)
        var = jnp.mean(jnp.square(o - mu), axis=-1, keepdims=True)
        on = (o - mu) * lax.rsqrt(var + GN_EPS)
        on = on * gng_ref[:, h * RET_DV:(h + 1) * RET_DV] + gnb_ref[:, h * RET_DV:(h + 1) * RET_DV]
        gh = g[:, h * RET_DV:(h + 1) * RET_DV]
        y_ref[0, :, h * RET_DV:(h + 1) * RET_DV] = gh * jax.nn.sigmoid(gh) * on

    @pl.when(c == pl.num_programs(1) - 1)
    def _():
        st_ref[0] = st_scr[...]


def _retention(z3, st0, cos2, sin2, gn_g, gn_b, C):
    b, t, _ = z3.shape
    nc = t // C
    vq = RET_HEADS * RET_DK
    vv = RET_HEADS * RET_DV
    return pl.pallas_call(
        functools.partial(_ret_kernel, C=C),
        grid=(b, nc),
        in_specs=[pl.BlockSpec((1, C, vq), lambda i, c: (i, c, COL_QR)),
                  pl.BlockSpec((1, C, vq), lambda i, c: (i, c, COL_KR)),
                  pl.BlockSpec((1, C, vv), lambda i, c: (i, c, COL_VR // 2)),
                  pl.BlockSpec((1, C, vv), lambda i, c: (i, c, COL_GR // 2)),
                  pl.BlockSpec((C, vq), lambda i, c: (c, 0)),
                  pl.BlockSpec((C, vq), lambda i, c: (c, 0)),
                  pl.BlockSpec((1, RET_HEADS, RET_DK, RET_DV), lambda i, c: (i, 0, 0, 0)),
                  pl.BlockSpec((1, vv), lambda i, c: (0, 0)),
                  pl.BlockSpec((1, vv), lambda i, c: (0, 0))],
        out_specs=[pl.BlockSpec((1, C, vv), lambda i, c: (i, c, 0)),
                   pl.BlockSpec((1, RET_HEADS, RET_DK, RET_DV), lambda i, c: (i, 0, 0, 0))],
        out_shape=[jax.ShapeDtypeStruct((b, t, vv), F32),
                   jax.ShapeDtypeStruct((b, RET_HEADS, RET_DK, RET_DV), F32)],
        scratch_shapes=[pltpu.VMEM((RET_HEADS, RET_DK, RET_DV), F32)],
        compiler_params=_cparams(("arbitrary", "arbitrary")),
        name="retention",
    )(z3, z3, z3, z3, cos2, sin2, st0, gn_g, gn_b)


def _rope_tables(pos):
    half = RET_DK // 2
    inv = 1.0 / (ROPE_BASE ** jnp.linspace(0.0, 1.0, half, dtype=F32))
    ang = pos.astype(F32)[:, None] * inv[None, :]
    cos, sin = jnp.cos(ang), jnp.sin(ang)
    cos2 = jnp.tile(jnp.concatenate([cos, cos], axis=-1), (1, RET_HEADS))
    sin2 = jnp.tile(jnp.concatenate([-sin, sin], axis=-1), (1, RET_HEADS))
    return cos2, sin2


def _pool_kernel(prev_ref, cur_ref, p_ref, *, tm, pos0, zero_first):
    i = pl.program_id(1)
    cur = cur_ref[0]
    prev = prev_ref[0]
    if zero_first:
        prev = jnp.where(i == 0, 0.0, prev)
    ext = jnp.concatenate([prev, cur], axis=0)
    row = lax.broadcasted_iota(jnp.int32, (tm, 1), 0) + i * tm + pos0
    acc = ext
    sums = []
    for s in (1, 2, 4, 8):
        acc = acc + pltpu.roll(acc, s, 0)
        sums.append(acc)
    for gi, w in enumerate(POOL_WINDOWS):
        sl = slice(gi * POOL_GW, (gi + 1) * POOL_GW)
        win = sums[gi][POOL_HALO:, sl]
        cnt = jnp.minimum(row + 1, w).astype(F32)
        p_ref[0, :, sl] = win / cnt - cur[:, sl]


def _pool(z3, prev, tm, pos0, zero_first):
    b, t, _ = z3.shape
    nt = t // tm
    width = 4 * POOL_GW
    if prev is None:
        prev_arr = z3
        hb = tm // POOL_HALO
        prev_spec = pl.BlockSpec((1, POOL_HALO, width), lambda bi, i: (bi, jnp.maximum(i * hb - 1, 0), COL_UP // 2))
    else:
        prev_arr = prev
        prev_spec = pl.BlockSpec((1, POOL_HALO, width), lambda bi, i: (bi, 0, 0))
    return pl.pallas_call(
        functools.partial(_pool_kernel, tm=tm, pos0=pos0, zero_first=zero_first),
        grid=(b, nt),
        in_specs=[prev_spec, pl.BlockSpec((1, tm, width), lambda bi, i: (bi, i, COL_UP // 2))],
        out_specs=pl.BlockSpec((1, tm, width), lambda bi, i: (bi, i, 0)),
        out_shape=jax.ShapeDtypeStruct((b, t, width), F32),
        compiler_params=_cparams(("arbitrary", "arbitrary")),
        name="pool",
    )(prev_arr, z3)


def _datt_kernel(q_ref, kc_ref, kp_ref, vc_ref, vp_ref, o_ref, l_ref):
    n = pl.program_id(2)
    qb = ATT_BLOCK
    q = q_ref[0] * (ATT_DH ** -0.5)
    kc, kp, vc, vp = kc_ref[0], kp_ref[0], vc_ref[0], vp_ref[0]
    ii = lax.broadcasted_iota(jnp.int32, (qb, qb), 0)
    jj = lax.broadcasted_iota(jnp.int32, (qb, qb), 1)
    mask_c = jj <= ii
    mask_p = (jj >= ii) & (n > 0)
    nt = (((1,), (1,)), ((), ()))
    for h in range(ATT_HEADS):
        sl = slice(h * ATT_DH, (h + 1) * ATT_DH)
        qh = q[:, sl].astype(BF16)
        s_c = jnp.where(mask_c, lax.dot_general(qh, kc[:, sl].astype(BF16), nt, preferred_element_type=F32), NEG)
        s_p = jnp.where(mask_p, lax.dot_general(qh, kp[:, sl].astype(BF16), nt, preferred_element_type=F32), NEG)
        m = jnp.maximum(jnp.max(s_c, axis=-1, keepdims=True), jnp.max(s_p, axis=-1, keepdims=True))
        p_c = jnp.exp(s_c - m)
        p_p = jnp.exp(s_p - m)
        den = jnp.sum(p_c, axis=-1, keepdims=True) + jnp.sum(p_p, axis=-1, keepdims=True)
        num = (jnp.dot(p_c.astype(BF16), vc[:, sl].astype(BF16), preferred_element_type=F32)
               + jnp.dot(p_p.astype(BF16), vp[:, sl].astype(BF16), preferred_element_type=F32))
        o_ref[0, :, sl] = num / den
        l_ref[0, :, sl] = jnp.broadcast_to(m + jnp.log(den), (qb, ATT_DH))


def _datt_prompt(z3, gi, dil):
    b, s, _ = z3.shape
    L = s // dil
    nb = L // ATT_BLOCK
    zr = z3.reshape(b, L, dil * IN_WIDTH)
    w = ATT_HEADS * ATT_DH

    def spec(col, prev):
        if prev:
            return pl.BlockSpec((1, ATT_BLOCK, w), lambda bi, r, n: (bi, jnp.maximum(n - 1, 0), r * N_CB + col + gi))
        return pl.BlockSpec((1, ATT_BLOCK, w), lambda bi, r, n: (bi, n, r * N_CB + col + gi))

    o, lse = pl.pallas_call(
        _datt_kernel,
        grid=(b, dil, nb),
        in_specs=[spec(COL_QA, False), spec(COL_KA, False), spec(COL_KA, True),
                  spec(COL_VA, False), spec(COL_VA, True)],
        out_specs=[pl.BlockSpec((1, ATT_BLOCK, w), lambda bi, r, n: (bi, n, r))] * 2,
        out_shape=[jax.ShapeDtypeStruct((b, L, dil * w), F32)] * 2,
        compiler_params=_cparams(("arbitrary", "arbitrary", "arbitrary")),
        name=f"datt_prompt_d{dil}",
    )(zr, zr, zr, zr, zr)
    return o.reshape(b * s, w), lse.reshape(b * s, w)


def _satt_kernel(q_ref, kn_ref, vn_ref, buf_ref, o_ref, l_ref, *, Wb, dil, T):
    w = ATT_HEADS * ATT_DH
    nr = ATT_HEADS * T
    q = q_ref[0] * (ATT_DH ** -0.5)
    rr = lax.broadcasted_iota(jnp.int32, (nr, w), 0)
    ll = lax.broadcasted_iota(jnp.int32, (nr, w), 1)
    qe = jnp.where(rr // T == ll // ATT_DH, jnp.concatenate([q] * ATT_HEADS, axis=0), 0.0).astype(BF16)
    kb = buf_ref[0, :, :w].astype(BF16)
    vb = buf_ref[0, :, w:].astype(BF16)
    kn = kn_ref[0].astype(BF16)
    vn = vn_ref[0].astype(BF16)
    nt = (((1,), (1,)), ((), ()))
    s_b = lax.dot_general(qe, kb, nt, preferred_element_type=F32)
    s_n = lax.dot_general(qe, kn, nt, preferred_element_type=F32)
    qi_b = lax.broadcasted_iota(jnp.int32, (nr, Wb), 0) % T
    dist_b = Wb + qi_b - lax.broadcasted_iota(jnp.int32, (nr, Wb), 1)
    ok_b = (dist_b % dil == 0) & (dist_b <= dil * ATT_BACK)
    qi_n = lax.broadcasted_iota(jnp.int32, (nr, T), 0) % T
    dist_n = qi_n - lax.broadcasted_iota(jnp.int32, (nr, T), 1)
    ok_n = (dist_n >= 0) & (dist_n % dil == 0)
    s_b = jnp.where(ok_b, s_b, NEG)
    s_n = jnp.where(ok_n, s_n, NEG)
    m = jnp.maximum(jnp.max(s_b, axis=-1, keepdims=True), jnp.max(s_n, axis=-1, keepdims=True))
    p_b = jnp.exp(s_b - m)
    p_n = jnp.exp(s_n - m)
    den = jnp.sum(p_b, axis=-1, keepdims=True) + jnp.sum(p_n, axis=-1, keepdims=True)
    num = (jnp.dot(p_b.astype(BF16), vb, preferred_element_type=F32)
           + jnp.dot(p_n.astype(BF16), vn, preferred_element_type=F32))
    o_all = num / den
    l_all = jnp.broadcast_to(m + jnp.log(den), (nr, w))
    lane_h = lax.broadcasted_iota(jnp.int32, (T, w), 1) // ATT_DH
    o = jnp.zeros((T, w), F32)
    l = jnp.zeros((T, w), F32)
    for h in range(ATT_HEADS):
        o = jnp.where(lane_h == h, o_all[h * T:(h + 1) * T], o)
        l = jnp.where(lane_h == h, l_all[h * T:(h + 1) * T], l)
    o_ref[0] = o
    l_ref[0] = l


def _datt_sample(z3, buf, gi, dil):
    b, t, _ = z3.shape
    Wb = buf.shape[1]
    w = ATT_HEADS * ATT_DH
    o, lse = pl.pallas_call(
        functools.partial(_satt_kernel, Wb=Wb, dil=dil, T=t),
        grid=(b,),
        in_specs=[pl.BlockSpec((1, t, w), lambda i: (i, 0, COL_QA + gi)),
                  pl.BlockSpec((1, t, w), lambda i: (i, 0, COL_KA + gi)),
                  pl.BlockSpec((1, t, w), lambda i: (i, 0, COL_VA + gi)),
                  pl.BlockSpec((1, Wb, 2 * w), lambda i: (i, 0, 0))],
        out_specs=[pl.BlockSpec((1, t, w), lambda i: (i, 0, 0))] * 2,
        out_shape=[jax.ShapeDtypeStruct((b, t, w), F32)] * 2,
        compiler_params=_cparams(("arbitrary",)),
        name=f"datt_sample_d{dil}",
    )(z3, z3, z3, buf)
    return o.reshape(b * t, w), lse.reshape(b * t, w)


def _mix_kernel(x_ref, ga_ref, yr_ref, p_ref, o1_ref, l1_ref, o2_ref, l2_ref, o3_ref, l3_ref,
                gr_ref, gp_ref, gatt_ref, wr_ref, wp_ref, ps_ref, wa_ref, wo_ref, out_ref):
    y_r = jnp.dot(yr_ref[...].astype(BF16), wr_ref[...], preferred_element_type=F32)
    p = p_ref[...]
    parts = []
    for gi in range(len(POOL_WINDOWS)):
        parts.append(jnp.dot(p[:, gi * POOL_GW:(gi + 1) * POOL_GW].astype(BF16), wp_ref[gi],
                             preferred_element_type=F32))
    y_p = jnp.concatenate(parts, axis=-1) * ps_ref[...]
    l1, l2, l3 = l1_ref[...], l2_ref[...], l3_ref[...]
    lm = jnp.maximum(jnp.maximum(l1, l2), l3)
    w1, w2, w3 = jnp.exp(l1 - lm), jnp.exp(l2 - lm), jnp.exp(l3 - lm)
    att = (w1 * o1_ref[...] + w2 * o2_ref[...] + w3 * o3_ref[...]) / (w1 + w2 + w3)
    y_a = jnp.dot(att.astype(BF16), wa_ref[...], preferred_element_type=F32)
    mix = (jax.nn.sigmoid(gr_ref[...]) * y_r + jax.nn.sigmoid(gp_ref[...]) * y_p
           + jax.nn.sigmoid(gatt_ref[...]) * y_a)
    out = jnp.dot(mix.astype(BF16), wo_ref[...], preferred_element_type=F32)
    out_ref[...] = x_ref[...] + ga_ref[0] * out


def _mix(x, ga, z, yr, p, att_parts, w_ret_out, pool_w, pool_scale, w_att_out, w_out, per_row, rows_per_batch):
    m = x.shape[0]
    tm = 256
    tpb = max(rows_per_batch // tm, 1)
    row = lambda width: pl.BlockSpec((tm, width), lambda i: (i, 0))
    full = lambda shape: pl.BlockSpec(shape, lambda i: (0,) * len(shape))
    gate = lambda k: pl.BlockSpec((tm, D), lambda i: (i, k))
    (o1, l1), (o2, l2), (o3, l3) = att_parts
    return pl.pallas_call(
        _mix_kernel,
        grid=(m // tm,),
        in_specs=[row(D), _mod_spec(per_row, tm, tpb, 0), row(512), row(512),
                  row(256), row(256), row(256), row(256), row(256), row(256),
                  gate(0), gate(1), gate(2),
                  full((512, D)), full((4, POOL_GW, 256)), full((1, D)), full((256, D)), full((D, D))],
        out_specs=row(D),
        out_shape=jax.ShapeDtypeStruct((m, D), F32),
        compiler_params=_cparams(("arbitrary",)),
        name="mix",
    )(x, ga, yr, p, o1, l1, o2, l2, o3, l3, z, z, z, w_ret_out, pool_w, pool_scale, w_att_out, w_out)


def _top_desc(s, k):
    vals = []
    for _ in range(k):
        mx = jnp.max(s, axis=0, keepdims=True)
        vals.append(mx)
        s = jnp.where(s == mx, NEG, s)
    return vals


def _route_kernel(x_ref, g_ref, sh_ref, sc_ref, wqt_ref, k1_ref, k2_ref,
                  ht_ref, e1_ref, e2_ref, thr_ref, *, tm):
    h2 = _rms_mod(x_ref[...], g_ref[...], sh_ref[0], sc_ref[0])
    ht = h2.T
    ht_ref[0] = ht.astype(BF16)
    qt = jnp.dot(wqt_ref[...], ht, preferred_element_type=F32, precision=lax.Precision.HIGHEST)
    half = PEER_NKEYS // 2
    for h in range(PEER_HEADS):
        base = h * PEER_NKEYS
        s1 = jnp.dot(k1_ref[h], qt[base:base + half], preferred_element_type=F32,
                     precision=lax.Precision.HIGHEST)
        s2 = jnp.dot(k2_ref[h], qt[base + half:base + PEER_NKEYS], preferred_element_type=F32,
                     precision=lax.Precision.HIGHEST)
        v1 = _top_desc(s1, PEER_TOPK)
        v2 = _top_desc(s2, PEER_TOPK)
        e1 = jnp.exp(s1 - v1[0])
        e2 = jnp.exp(s2 - v2[0])
        ev1 = [jnp.exp(v - v1[0]) for v in v1]
        ev2 = [jnp.exp(v - v2[0]) for v in v2]
        cand = jnp.concatenate([ev1[a] * ev2[b] for a, b in PEER_PAIRS], axis=0)
        rem = cand
        z = jnp.zeros((1, tm), F32)
        thr_u = z
        for _ in range(PEER_TOPK):
            mx = jnp.max(rem, axis=0, keepdims=True)
            z = z + mx
            thr_u = mx
            rem = jnp.where(rem == mx, -1.0, rem)
        rz = 1.0 / z
        candn = jnp.concatenate([(ev1[a] * rz) * ev2[b] for a, b in PEER_PAIRS], axis=0)
        thr_n = jnp.min(jnp.where(cand >= thr_u, candn, jnp.inf), axis=0, keepdims=True)
        e1_ref[0, base:base + PEER_NKEYS, :] = e1 * rz
        e2_ref[0, base:base + PEER_NKEYS, :] = e2
        thr_ref[0, h:h + 1, :] = thr_n


def _route(x, g, sh, sc, wq_t, k1, k2, per_row, rows_per_batch, tm):
    m = x.shape[0]
    nt = m // tm
    tpb = max(rows_per_batch // tm, 1)
    hk = PEER_HEADS * PEER_NKEYS
    full = lambda shape: pl.BlockSpec(shape, lambda i: (0,) * len(shape))
    tile = lambda r: pl.BlockSpec((1, r, tm), lambda i: (i, 0, 0))
    return pl.pallas_call(
        functools.partial(_route_kernel, tm=tm),
        grid=(nt,),
        in_specs=[pl.BlockSpec((tm, D), lambda i: (i, 0)), full((1, D)),
                  _mod_spec(per_row, tm, tpb, 0), _mod_spec(per_row, tm, tpb, 0),
                  full((D, D)), full(k1.shape), full(k2.shape)],
        out_specs=[tile(D), tile(hk), tile(hk), tile(PEER_HEADS)],
        out_shape=[jax.ShapeDtypeStruct((nt, D, tm), BF16),
                   jax.ShapeDtypeStruct((nt, hk, tm), F32),
                   jax.ShapeDtypeStruct((nt, hk, tm), F32),
                   jax.ShapeDtypeStruct((nt, PEER_HEADS, tm), F32)],
        compiler_params=_cparams(("arbitrary",)),
        name="peer_route",
    )(x, g, sh, sc, wq_t, k1, k2)


def _gelu_tanh(x):
    return 0.5 * x * (1.0 + jnp.tanh(0.7978845608028654 * (x + 0.044715 * (x * x * x))))


def _experts_kernel(ht_ref, e1_ref, e2_ref, thr_ref, u_ref, vt_ref, x_ref, ga_ref, fg_ref,
                    out_ref, acc_ref, hid_ref, *, tm, te, final):
    e = pl.program_id(1)

    @pl.when(e == 0)
    def _():
        acc_ref[...] = jnp.zeros_like(acc_ref)

    a_t = jnp.dot(u_ref[...], ht_ref[0], preferred_element_type=F32)
    lc = 128
    n_i1 = te // PEER_NKEYS
    for r in range(n_i1):
        for c in range(tm // lc):
            cs = slice(c * lc, (c + 1) * lc)
            w = jnp.zeros((PEER_NKEYS, lc), F32)
            for h in range(PEER_HEADS):
                start = pl.multiple_of(h * PEER_NKEYS + e * n_i1, n_i1)
                e1row = e1_ref[0, pl.ds(start, n_i1), cs][r:r + 1]
                p = e1row * e2_ref[0, h * PEER_NKEYS:(h + 1) * PEER_NKEYS, cs]
                w = w + jnp.where(p >= thr_ref[0, h:h + 1, cs], p, 0.0)
            a = a_t[r * PEER_NKEYS:(r + 1) * PEER_NKEYS, cs]
            hid_ref[r * PEER_NKEYS:(r + 1) * PEER_NKEYS, cs] = (w * _gelu_tanh(a)).astype(BF16)
    acc_ref[...] += jnp.dot(vt_ref[...], hid_ref[...], preferred_element_type=F32)

    @pl.when(e == pl.num_programs(1) - 1)
    def _():
        y = x_ref[...] + ga_ref[0] * acc_ref[...].T
        if final:
            ms = jnp.mean(y * y, axis=-1, keepdims=True)
            y = y * lax.rsqrt(ms + RMS_EPS) * fg_ref[...]
        out_ref[...] = y


def _experts(ht, e1, e2, thr, u_bf, vt_bf, x, ga, fg, per_row, rows_per_batch, tm, te, final):
    m = x.shape[0]
    nt = m // tm
    ne = PEER_EXPERTS // te
    tpb = max(rows_per_batch // tm, 1)
    hk = PEER_HEADS * PEER_NKEYS
    tile = lambda r: pl.BlockSpec((1, r, tm), lambda i, e: (i, 0, 0))
    return pl.pallas_call(
        functools.partial(_experts_kernel, tm=tm, te=te, final=final),
        grid=(nt, ne),
        in_specs=[tile(D), tile(hk), tile(hk), tile(PEER_HEADS),
                  pl.BlockSpec((te, D), lambda i, e: (e, 0)),
                  pl.BlockSpec((D, te), lambda i, e: (0, e)),
                  pl.BlockSpec((tm, D), lambda i, e: (i, 0)),
                  _mod_spec(per_row, tm, tpb, 0),
                  pl.BlockSpec((1, D), lambda i, e: (0, 0))],
        out_specs=pl.BlockSpec((tm, D), lambda i, e: (i, 0)),
        out_shape=jax.ShapeDtypeStruct((m, D), F32),
        scratch_shapes=[pltpu.VMEM((D, tm), F32), pltpu.VMEM((te, tm), BF16)],
        compiler_params=_cparams(("arbitrary", "arbitrary")),
        name="peer_experts",
    )(ht, e1, e2, thr, u_bf, vt_bf, x, ga, fg)


def _group_rows(per_row, vec, rows_per_batch, tm):
    if per_row:
        rows = jnp.repeat(vec, rows_per_batch, axis=0)
        return rows.reshape(rows.shape[0] // tm, tm, D)
    return vec[:, None, :]


def _trunk_layer(x, mods, lw, per_row, b, t, ret_st0, pool_prev, bufs, pos, final, final_g):
    sh1, sc1, ga1, sh2, sc2, ga2 = mods
    m = b * t
    gm = lambda v, tm: _group_rows(per_row, v, t, tm)
    z = _inproj(x, lw["g_mix"], gm(sh1, 512), gm(sc1, 512), lw["w_in"], per_row, t)
    z3 = z.reshape(b, t, IN_WIDTH)

    cos2, sin2 = _rope_tables(pos)
    chunk = RET_CHUNK if t >= RET_CHUNK else t
    yr, new_ret = _retention(z3, ret_st0, cos2, sin2, lw["gn_g"], lw["gn_b"], chunk)

    if per_row:
        p = _pool(z3, pool_prev, t, PAST_LEN, False)
    else:
        p = _pool(z3, None, 512, 0, True)

    att_parts = []
    for gi, (win, dil) in enumerate(ATT_PATTERNS):
        if per_row:
            att_parts.append(_datt_sample(z3, bufs[gi], gi, dil))
        else:
            att_parts.append(_datt_prompt(z3, gi, dil))

    x1 = _mix(x, gm(ga1, 256), z, yr.reshape(m, 512), p.reshape(m, 512), att_parts,
              lw["w_ret_out"], lw["pool_w"], lw["pool_scale"], lw["w_att_out"], lw["w_out"], per_row, t)

    tm = 512
    ht, e1, e2, thr = _route(x1, lw["g_ffn"], gm(sh2, tm), gm(sc2, tm), lw["wq_t"], lw["k1"], lw["k2"],
                             per_row, t, tm)
    x2 = _experts(ht, e1, e2, thr, lw["u"], lw["v_t"], x1, gm(ga2, tm), final_g, per_row, t, tm, 1024, final)

    w = ATT_HEADS * ATT_DH
    new_kv = []
    for gi, (win, dil) in enumerate(ATT_PATTERNS):
        keep = t if per_row else min(win, t)
        kk = z3[:, t - keep:, (COL_KA + gi) * CB:(COL_KA + gi + 1) * CB].reshape(b, keep, ATT_HEADS, ATT_DH)
        vv = z3[:, t - keep:, (COL_VA + gi) * CB:(COL_VA + gi + 1) * CB].reshape(b, keep, ATT_HEADS, ATT_DH)
        new_kv.append(jnp.stack([kk, vv], axis=2))
    up = z3[:, :, COL_UP * CB:COL_UP * CB + 512]
    if per_row:
        new_pool = jnp.concatenate([pool_prev[:, 1:], up], axis=1)[:, -(POOL_HALO - 1):]
    else:
        new_pool = up[:, t - (POOL_HALO - 1):]
    return x2, (new_ret, new_pool, new_kv)


def kernel(x_prompt, x_sample, state_ret, state_pool, cache_kv_w128, cache_kv_w512, cache_kv_w2048,
           c_prompt, c_sample, ada_w, ada_b, norm_mix_g, norm_ffn_g, w_in, ret_gn_g, ret_gn_b,
           w_ret_out, pool_w, pool_scale, w_att_out, w_out, peer_wq, peer_keys, peer_u, peer_v,
           final_norm_g):
    depth = ada_w.shape[0]
    bp, s, _ = x_prompt.shape
    bs, t, _ = x_sample.shape
    caches = (cache_kv_w128, cache_kv_w512, cache_kv_w2048)

    mods_all = _adaln(jnp.concatenate([c_prompt, c_sample], axis=0), ada_w, ada_b)

    xp = x_prompt.reshape(bp * s, D)
    xs = x_sample.reshape(bs * t, D)
    final_g = final_norm_g.reshape(1, D)
    zero_state = jnp.zeros((bp, RET_HEADS, RET_DK, RET_DV), F32)
    outs_p, outs_s = [], []
    for l in range(depth):
        n_gate = 3 * D
        w_in_l = jnp.concatenate([w_in[l][:, IN_WIDTH - n_gate:], w_in[l][:, :IN_WIDTH - n_gate]], axis=1)
        lw = {
            "g_mix": norm_mix_g[l].reshape(1, D), "g_ffn": norm_ffn_g[l].reshape(1, D),
            "w_in": w_in_l.astype(BF16),
            "gn_g": ret_gn_g[l].reshape(1, -1), "gn_b": ret_gn_b[l].reshape(1, -1),
            "w_ret_out": w_ret_out[l].astype(BF16), "pool_w": pool_w[l].astype(BF16),
            "pool_scale": pool_scale[l].reshape(1, D), "w_att_out": w_att_out[l].astype(BF16),
            "w_out": w_out[l].astype(BF16),
            "wq_t": peer_wq[l].T, "k1": peer_keys[l][:, 0], "k2": peer_keys[l][:, 1],
            "u": peer_u[l].astype(BF16), "v_t": peer_v[l].T.astype(BF16),
        }
        ml = mods_all[l]
        mods_p = tuple(ml[:bp, k * D:(k + 1) * D] for k in range(6))
        mods_s = tuple(ml[bp:, k * D:(k + 1) * D] for k in range(6))
        final = l == depth - 1
        xp, st_p = _trunk_layer(xp, mods_p, lw, False, bp, s, zero_state, None, None,
                                jnp.arange(s), final, final_g)
        pool_prev = jnp.pad(state_pool[l], ((0, 0), (1, 0), (0, 0)))
        bufs = tuple(c[l].reshape(bs, c.shape[2], 2 * ATT_HEADS * ATT_DH) for c in caches)
        xs, st_s = _trunk_layer(xs, mods_s, lw, True, bs, t, state_ret[l], pool_prev, bufs,
                                PAST_LEN + jnp.arange(t), final, final_g)
        outs_p.append(st_p)
        outs_s.append(st_s)

    def stack(outs, pick):
        return jnp.stack([pick(o) for o in outs])

    res = [xp.reshape(bp, s, D), xs.reshape(bs, t, D),
           stack(outs_p, lambda o: o[0]), stack(outs_s, lambda o: o[0]),
           stack(outs_p, lambda o: o[1]), stack(outs_s, lambda o: o[1])]
    for gi in range(len(ATT_PATTERNS)):
        res.append(stack(outs_p, lambda o: o[2][gi]))
        res.append(stack(outs_s, lambda o: o[2][gi]))
    return tuple(res)
```

```python
import functools
import math

import jax
import jax.numpy as jnp
from jax import lax
from jax.experimental import pallas as pl
from jax.experimental.pallas import tpu as pltpu

F32 = jnp.float32
BF16 = jnp.bfloat16

D = 1024
RMS_EPS = 1e-6
GN_EPS = 1e-5
PAST_LEN = 2048

RET_HEADS = 4
RET_DK = 64
RET_DV = 128
RET_CHUNK = 128
ROPE_BASE = 10000.0
RET_LOG_GAMMA = tuple(math.log(1.0 - 2.0 ** (-5.0 - h)) for h in range(RET_HEADS))

POOL_WINDOWS = (2, 4, 8, 16)
POOL_GW = 128
POOL_HALO = 16

ATT_PATTERNS = ((128, 1), (512, 4), (2048, 16))
ATT_HEADS = 4
ATT_DH = 64
ATT_BLOCK = 128
ATT_PAIR = 2
ATT_BACK = 128
NEG = -1e30

PEER_HEADS = 8
PEER_NKEYS = 128
PEER_EXPERTS = PEER_NKEYS * PEER_NKEYS
PEER_TOPK = 16

IN_WIDTH = 7424
CB = 256
COL_GATE = 0
COL_QR, COL_KR, COL_VR, COL_GR, COL_UP = 12, 13, 14, 16, 18
COL_QA, COL_KA, COL_VA = 20, 23, 26
N_CB = IN_WIDTH // CB

VMEM_LIMIT = 56 * 1024 * 1024


def _cparams(sem):
    return pltpu.CompilerParams(dimension_semantics=sem, vmem_limit_bytes=VMEM_LIMIT)


def _mod_spec(per_row, tm, tiles_per_batch, grid_pos):
    if per_row:
        return pl.BlockSpec((1, tm, D), lambda *g: (g[grid_pos], 0, 0))
    return pl.BlockSpec((1, 1, D), lambda *g: (g[grid_pos] // tiles_per_batch, 0, 0))


def _rms_mod(x, g, sh, sc):
    ms = jnp.mean(x * x, axis=-1, keepdims=True)
    h = x * lax.rsqrt(ms + RMS_EPS) * g
    return h * (1.0 + sc) + sh


def _ada_kernel(c_ref, w_ref, b_ref, o_ref):
    c = c_ref[...]
    s = c * jax.nn.sigmoid(c)
    o_ref[0] = jnp.dot(s.astype(BF16), w_ref[0].astype(BF16), preferred_element_type=F32) + b_ref[0]


def _adaln(c_all, ada_w, ada_b):
    depth, _, n = ada_w.shape
    m = c_all.shape[0]
    tn = 512
    return pl.pallas_call(
        _ada_kernel,
        grid=(depth, n // tn),
        in_specs=[pl.BlockSpec((m, D), lambda l, j: (0, 0)),
                  pl.BlockSpec((1, D, tn), lambda l, j: (l, 0, j)),
                  pl.BlockSpec((1, 1, tn), lambda l, j: (l, 0, j))],
        out_specs=pl.BlockSpec((1, m, tn), lambda l, j: (l, 0, j)),
        out_shape=jax.ShapeDtypeStruct((depth, m, n), F32),
        compiler_params=_cparams(("arbitrary", "arbitrary")),
        name="adaln",
    )(c_all, ada_w, ada_b.reshape(depth, 1, n))


def _inproj_kernel(x_ref, g_ref, sh_ref, sc_ref, w_ref, o_ref):
    h = _rms_mod(x_ref[...], g_ref[...], sh_ref[0], sc_ref[0])
    o_ref[...] = jnp.dot(h.astype(BF16), w_ref[...], preferred_element_type=F32)


def _inproj(x, g, sh, sc, w_bf, per_row, rows_per_batch):
    m = x.shape[0]
    tm = 512
    tn = IN_WIDTH // 2
    tpb = max(rows_per_batch // tm, 1)
    return pl.pallas_call(
        _inproj_kernel,
        grid=(2, m // tm),
        in_specs=[pl.BlockSpec((tm, D), lambda j, i: (i, 0)),
                  pl.BlockSpec((1, D), lambda j, i: (0, 0)),
                  _mod_spec(per_row, tm, tpb, 1),
                  _mod_spec(per_row, tm, tpb, 1),
                  pl.BlockSpec((D, tn), lambda j, i: (0, j))],
        out_specs=pl.BlockSpec((tm, tn), lambda j, i: (i, j)),
        out_shape=jax.ShapeDtypeStruct((m, IN_WIDTH), F32),
        compiler_params=_cparams(("arbitrary", "arbitrary")),
        name="inproj",
    )(x, g, sh, sc, w_bf)


def _ret_kernel(q_ref, k_ref, v_ref, g_ref, cos_ref, sin_ref, st0_ref, gng_ref, gnb_ref,
                y_ref, st_ref, st_scr, *, C):
    c = pl.program_id(1)

    @pl.when(c == 0)
    def _():
        st_scr[...] = st0_ref[0]

    cos = cos_ref[...]
    sin = sin_ref[...]
    lane = lax.broadcasted_iota(jnp.int32, (C, RET_HEADS * RET_DK), 1)
    first_half = (lane % RET_DK) < (RET_DK // 2)

    def rot(x):
        nl = RET_HEADS * RET_DK
        swapped = jnp.where(first_half, pltpu.roll(x, nl - RET_DK // 2, 1), pltpu.roll(x, RET_DK // 2, 1))
        return x * cos + swapped * sin

    q = rot(q_ref[0])
    k = rot(k_ref[0]) * (RET_DK ** -0.5)
    v = v_ref[0]
    g = g_ref[0]
    ii = lax.broadcasted_iota(jnp.int32, (C, C), 0)
    jj = lax.broadcasted_iota(jnp.int32, (C, C), 1)
    diff = (ii - jj).astype(F32)
    causal = ii >= jj
    row = lax.broadcasted_iota(jnp.int32, (C, 1), 0).astype(F32)
    for h in range(RET_HEADS):
        lg = RET_LOG_GAMMA[h]
        qh = q[:, h * RET_DK:(h + 1) * RET_DK].astype(BF16)
        khf = k[:, h * RET_DK:(h + 1) * RET_DK]
        vh = v[:, h * RET_DV:(h + 1) * RET_DV].astype(BF16)
        st = st_scr[h]
        decay = jnp.where(causal, jnp.exp(lg * jnp.maximum(diff, 0.0)), 0.0)
        s = lax.dot_general(qh, khf.astype(BF16), (((1,), (1,)), ((), ())), preferred_element_type=F32) * decay
        o = jnp.dot(s.astype(BF16), vh, preferred_element_type=F32)
        o = o + jnp.dot(qh, st.astype(BF16), preferred_element_type=F32) * jnp.exp(lg * (row + 1.0))
        kdec = (khf * jnp.exp(lg * (C - 1.0 - row))).astype(BF16)
        st_scr[h] = math.exp(lg * C) * st + lax.dot_general(
            kdec, vh, (((0,), (0,)), ((), ())), preferred_element_type=F32)
        mu = jnp.mean(o, axis=-1, keepdims=True)
        var = jnp.mean(jnp.square(o - mu), axis=-1, keepdims=True)
        on = (o - mu) * lax.rsqrt(var + GN_EPS)
        on = on * gng_ref[:, h * RET_DV:(h + 1) * RET_DV] + gnb_ref[:, h * RET_DV:(h + 1) * RET_DV]
        gh = g[:, h * RET_DV:(h + 1) * RET_DV]
        y_ref[0, :, h * RET_DV:(h + 1) * RET_DV] = gh * jax.nn.sigmoid(gh) * on

    @pl.when(c == pl.num_programs(1) - 1)
    def _():
        st_ref[0] = st_scr[...]


def _retention(z3, st0, cos2, sin2, gn_g, gn_b, C):
    b, t, _ = z3.shape
    nc = t // C
    vq = RET_HEADS * RET_DK
    vv = RET_HEADS * RET_DV
    return pl.pallas_call(
        functools.partial(_ret_kernel, C=C),
        grid=(b, nc),
        in_specs=[pl.BlockSpec((1, C, vq), lambda i, c: (i, c, COL_QR)),
                  pl.BlockSpec((1, C, vq), lambda i, c: (i, c, COL_KR)),
                  pl.BlockSpec((1, C, vv), lambda i, c: (i, c, COL_VR // 2)),
                  pl.BlockSpec((1, C, vv), lambda i, c: (i, c, COL_GR // 2)),
                  pl.BlockSpec((C, vq), lambda i, c: (c, 0)),
                  pl.BlockSpec((C, vq), lambda i, c: (c, 0)),
                  pl.BlockSpec((1, RET_HEADS, RET_DK, RET_DV), lambda i, c: (i, 0, 0, 0)),
                  pl.BlockSpec((1, vv), lambda i, c: (0, 0)),
                  pl.BlockSpec((1, vv), lambda i, c: (0, 0))],
        out_specs=[pl.BlockSpec((1, C, vv), lambda i, c: (i, c, 0)),
                   pl.BlockSpec((1, RET_HEADS, RET_DK, RET_DV), lambda i, c: (i, 0, 0, 0))],
        out_shape=[jax.ShapeDtypeStruct((b, t, vv), F32),
                   jax.ShapeDtypeStruct((b, RET_HEADS, RET_DK, RET_DV), F32)],
        scratch_shapes=[pltpu.VMEM((RET_HEADS, RET_DK, RET_DV), F32)],
        compiler_params=_cparams(("arbitrary", "arbitrary")),
        name="retention",
    )(z3, z3, z3, z3, cos2, sin2, st0, gn_g, gn_b)


def _rope_tables(pos):
    half = RET_DK // 2
    inv = 1.0 / (ROPE_BASE ** jnp.linspace(0.0, 1.0, half, dtype=F32))
    ang = pos.astype(F32)[:, None] * inv[None, :]
    cos, sin = jnp.cos(ang), jnp.sin(ang)
    cos2 = jnp.tile(jnp.concatenate([cos, cos], axis=-1), (1, RET_HEADS))
    sin2 = jnp.tile(jnp.concatenate([-sin, sin], axis=-1), (1, RET_HEADS))
    return cos2, sin2


def _pool_kernel(prev_ref, cur_ref, p_ref, *, tm, pos0, zero_first):
    i = pl.program_id(1)
    cur = cur_ref[0]
    prev = prev_ref[0]
    if zero_first:
        prev = jnp.where(i == 0, 0.0, prev)
    ext = jnp.concatenate([prev, cur], axis=0)
    row = lax.broadcasted_iota(jnp.int32, (tm, 1), 0) + i * tm + pos0
    acc = ext
    sums = []
    for s in (1, 2, 4, 8):
        acc = acc + pltpu.roll(acc, s, 0)
        sums.append(acc)
    for gi, w in enumerate(POOL_WINDOWS):
        sl = slice(gi * POOL_GW, (gi + 1) * POOL_GW)
        win = sums[gi][POOL_HALO:, sl]
        cnt = jnp.minimum(row + 1, w).astype(F32)
        p_ref[0, :, sl] = win / cnt - cur[:, sl]


def _pool(z3, prev, tm, pos0, zero_first):
    b, t, _ = z3.shape
    nt = t // tm
    width = 4 * POOL_GW
    if prev is None:
        prev_arr = z3
        hb = tm // POOL_HALO
        prev_spec = pl.BlockSpec((1, POOL_HALO, width), lambda bi, i: (bi, jnp.maximum(i * hb - 1, 0), COL_UP // 2))
    else:
        prev_arr = prev
        prev_spec = pl.BlockSpec((1, POOL_HALO, width), lambda bi, i: (bi, 0, 0))
    return pl.pallas_call(
        functools.partial(_pool_kernel, tm=tm, pos0=pos0, zero_first=zero_first),
        grid=(b, nt),
        in_specs=[prev_spec, pl.BlockSpec((1, tm, width), lambda bi, i: (bi, i, COL_UP // 2))],
        out_specs=pl.BlockSpec((1, tm, width), lambda bi, i: (bi, i, 0)),
        out_shape=jax.ShapeDtypeStruct((b, t, width), F32),
        compiler_params=_cparams(("arbitrary", "arbitrary")),
        name="pool",
    )(prev_arr, z3)


def _datt_kernel(q_ref, kc_ref, kp_ref, vc_ref, vp_ref, o_ref, l_ref, *, dil):
    n = pl.program_id(1)
    qb = ATT_BLOCK
    ii = lax.broadcasted_iota(jnp.int32, (qb, qb), 0)
    jj = lax.broadcasted_iota(jnp.int32, (qb, qb), 1)
    mask_c = jj <= ii
    mask_p = (jj >= ii) & (n > 0)
    nt = (((1,), (1,)), ((), ()))

    def residue(r, carry):
        rows = pl.ds(r, qb, stride=dil) if dil > 1 else pl.ds(0, qb)
        q = q_ref[0, rows, :] * (ATT_DH ** -0.5)
        kc, kp, vc, vp = kc_ref[0, rows, :], kp_ref[0, rows, :], vc_ref[0, rows, :], vp_ref[0, rows, :]
        outs, lses = [], []
        for h in range(ATT_PAIR):
            sl = slice(h * ATT_DH, (h + 1) * ATT_DH)
            qh = q[:, sl].astype(BF16)
            s_c = jnp.where(mask_c, lax.dot_general(qh, kc[:, sl].astype(BF16), nt, preferred_element_type=F32), NEG)
            s_p = jnp.where(mask_p, lax.dot_general(qh, kp[:, sl].astype(BF16), nt, preferred_element_type=F32), NEG)
            m = jnp.maximum(jnp.max(s_c, axis=-1, keepdims=True), jnp.max(s_p, axis=-1, keepdims=True))
            p_c = jnp.exp(s_c - m)
            p_p = jnp.exp(s_p - m)
            den = jnp.sum(p_c, axis=-1, keepdims=True) + jnp.sum(p_p, axis=-1, keepdims=True)
            num = (jnp.dot(p_c.astype(BF16), vc[:, sl].astype(BF16), preferred_element_type=F32)
                   + jnp.dot(p_p.astype(BF16), vp[:, sl].astype(BF16), preferred_element_type=F32))
            outs.append(num / den)
            lses.append(jnp.broadcast_to(m + jnp.log(den), (qb, ATT_DH)))
        o_ref[0, rows, :] = jnp.concatenate(outs, axis=-1)
        l_ref[0, rows, :] = jnp.concatenate(lses, axis=-1)
        return carry

    if dil == 1:
        residue(0, 0)
    else:
        lax.fori_loop(0, dil, residue, 0)


def _datt_prompt(z3, gi, dil):
    b, s, _ = z3.shape
    tb = ATT_BLOCK * dil
    w = ATT_HEADS * ATT_DH
    wp = ATT_PAIR * ATT_DH
    npair = ATT_HEADS // ATT_PAIR

    def spec(col, prev):
        if prev:
            return pl.BlockSpec((1, tb, wp), lambda bi, n, hp: (bi, jnp.maximum(n - 1, 0), (col + gi) * npair + hp))
        return pl.BlockSpec((1, tb, wp), lambda bi, n, hp: (bi, n, (col + gi) * npair + hp))

    o, lse = pl.pallas_call(
        functools.partial(_datt_kernel, dil=dil),
        grid=(b, s // tb, npair),
        in_specs=[spec(COL_QA, False), spec(COL_KA, False), spec(COL_KA, True),
                  spec(COL_VA, False), spec(COL_VA, True)],
        out_specs=[pl.BlockSpec((1, tb, wp), lambda bi, n, hp: (bi, n, hp))] * 2,
        out_shape=[jax.ShapeDtypeStruct((b, s, w), F32)] * 2,
        compiler_params=_cparams(("arbitrary", "arbitrary", "arbitrary")),
        name=f"datt_prompt_d{dil}",
    )(z3, z3, z3, z3, z3)
    return o.reshape(b * s, w), lse.reshape(b * s, w)


def _satt_kernel(*refs, T):
    ng = len(ATT_PATTERNS)
    qkv = refs[:3 * ng]
    bufs = refs[3 * ng:4 * ng]
    outs = refs[4 * ng:]
    nt = (((1,), (1,)), ((), ()))
    for gi, (win, dil) in enumerate(ATT_PATTERNS):
        q = qkv[gi][0] * (ATT_DH ** -0.5)
        kn = qkv[ng + gi][0]
        vn = qkv[2 * ng + gi][0]
        buf = bufs[gi]
        Wb = buf.shape[-1]
        qi_b = lax.broadcasted_iota(jnp.int32, (T, Wb), 0)
        dist_b = Wb + qi_b - lax.broadcasted_iota(jnp.int32, (T, Wb), 1)
        ok_b = (dist_b % dil == 0) & (dist_b <= dil * ATT_BACK)
        dist_n = lax.broadcasted_iota(jnp.int32, (T, T), 0) - lax.broadcasted_iota(jnp.int32, (T, T), 1)
        ok_n = (dist_n >= 0) & (dist_n % dil == 0)
        o_parts, l_parts = [], []
        for h in range(ATT_HEADS):
            sl = slice(h * ATT_DH, (h + 1) * ATT_DH)
            qh = q[:, sl].astype(BF16)
            kt = buf[0, 0, h].astype(BF16)
            vt = buf[0, 1, h].astype(BF16)
            s_b = jnp.where(ok_b, jnp.dot(qh, kt, preferred_element_type=F32), NEG)
            s_n = jnp.where(ok_n, lax.dot_general(qh, kn[:, sl].astype(BF16), nt, preferred_element_type=F32), NEG)
            m = jnp.maximum(jnp.max(s_b, axis=-1, keepdims=True), jnp.max(s_n, axis=-1, keepdims=True))
            p_b = jnp.exp(s_b - m)
            p_n = jnp.exp(s_n - m)
            den = jnp.sum(p_b, axis=-1, keepdims=True) + jnp.sum(p_n, axis=-1, keepdims=True)
            num = (lax.dot_general(p_b.astype(BF16), vt, nt, preferred_element_type=F32)
                   + jnp.dot(p_n.astype(BF16), vn[:, sl].astype(BF16), preferred_element_type=F32))
            o_parts.append(num / den)
            l_parts.append(jnp.broadcast_to(m + jnp.log(den), (T, ATT_DH)))
        outs[2 * gi][0] = jnp.concatenate(o_parts, axis=-1)
        outs[2 * gi + 1][0] = jnp.concatenate(l_parts, axis=-1)


def _datt_sample(z3, bufs_t, layer):
    b, t, _ = z3.shape
    w = ATT_HEADS * ATT_DH
    ng = len(ATT_PATTERNS)
    col = lambda c: pl.BlockSpec((1, t, w), lambda i: (i, 0, c))
    in_specs = ([col(COL_QA + gi) for gi in range(ng)] + [col(COL_KA + gi) for gi in range(ng)]
                + [col(COL_VA + gi) for gi in range(ng)]
                + [pl.BlockSpec((None, 1) + bt.shape[2:], lambda i: (layer, i, 0, 0, 0, 0)) for bt in bufs_t])
    res = pl.pallas_call(
        functools.partial(_satt_kernel, T=t),
        grid=(b,),
        in_specs=in_specs,
        out_specs=[pl.BlockSpec((1, t, w), lambda i: (i, 0, 0))] * (2 * ng),
        out_shape=[jax.ShapeDtypeStruct((b, t, w), F32)] * (2 * ng),
        compiler_params=_cparams(("arbitrary",)),
        name="datt_sample",
    )(*([z3] * (3 * ng)), *bufs_t)
    return [(res[2 * gi].reshape(b * t, w), res[2 * gi + 1].reshape(b * t, w)) for gi in range(ng)]


def _mix_kernel(x_ref, ga_ref, yr_ref, p_ref, o1_ref, l1_ref, o2_ref, l2_ref, o3_ref, l3_ref,
                gr_ref, gp_ref, gatt_ref, wr_ref, wp_ref, ps_ref, wa_ref, wo_ref, out_ref):
    y_r = jnp.dot(yr_ref[...].astype(BF16), wr_ref[...], preferred_element_type=F32)
    p = p_ref[...]
    parts = []
    for gi in range(len(POOL_WINDOWS)):
        parts.append(jnp.dot(p[:, gi * POOL_GW:(gi + 1) * POOL_GW].astype(BF16), wp_ref[gi],
                             preferred_element_type=F32))
    y_p = jnp.concatenate(parts, axis=-1) * ps_ref[...]
    l1, l2, l3 = l1_ref[...], l2_ref[...], l3_ref[...]
    lm = jnp.maximum(jnp.maximum(l1, l2), l3)
    w1, w2, w3 = jnp.exp(l1 - lm), jnp.exp(l2 - lm), jnp.exp(l3 - lm)
    att = (w1 * o1_ref[...] + w2 * o2_ref[...] + w3 * o3_ref[...]) / (w1 + w2 + w3)
    y_a = jnp.dot(att.astype(BF16), wa_ref[...], preferred_element_type=F32)
    mix = (jax.nn.sigmoid(gr_ref[...]) * y_r + jax.nn.sigmoid(gp_ref[...]) * y_p
           + jax.nn.sigmoid(gatt_ref[...]) * y_a)
    out = jnp.dot(mix.astype(BF16), wo_ref[...], preferred_element_type=F32)
    out_ref[...] = x_ref[...] + ga_ref[0] * out


def _mix(x, ga, z, yr, p, att_parts, w_ret_out, pool_w, pool_scale, w_att_out, w_out, per_row, rows_per_batch):
    m = x.shape[0]
    tm = 256
    tpb = max(rows_per_batch // tm, 1)
    row = lambda width: pl.BlockSpec((tm, width), lambda i: (i, 0))
    full = lambda shape: pl.BlockSpec(shape, lambda i: (0,) * len(shape))
    gate = lambda k: pl.BlockSpec((tm, D), lambda i: (i, k))
    (o1, l1), (o2, l2), (o3, l3) = att_parts
    return pl.pallas_call(
        _mix_kernel,
        grid=(m // tm,),
        in_specs=[row(D), _mod_spec(per_row, tm, tpb, 0), row(512), row(512),
                  row(256), row(256), row(256), row(256), row(256), row(256),
                  gate(0), gate(1), gate(2),
                  full((512, D)), full((4, POOL_GW, 256)), full((1, D)), full((256, D)), full((D, D))],
        out_specs=row(D),
        out_shape=jax.ShapeDtypeStruct((m, D), F32),
        compiler_params=_cparams(("arbitrary",)),
        name="mix",
    )(x, ga, yr, p, o1, l1, o2, l2, o3, l3, z, z, z, w_ret_out, pool_w, pool_scale, w_att_out, w_out)


def _split_bf16(x):
    hi = x.astype(BF16)
    return hi, (x - hi.astype(F32)).astype(BF16)


def _dot3(a_hi, a_lo, b_hi, b_lo):
    d = functools.partial(jnp.dot, preferred_element_type=F32)
    return d(a_hi, b_hi) + (d(a_hi, b_lo) + d(a_lo, b_hi))


def _twice_bf16(x):
    u = pltpu.bitcast(x.astype(BF16).astype(F32), jnp.uint32)
    return pltpu.bitcast(u | (u >> 16), F32)


def _extract_top(s, v_scr, want_rank):
    rank = jnp.full(s.shape, 127.0, F32)
    for k in range(PEER_TOPK):
        mx = jnp.max(s, axis=0, keepdims=True)
        v_scr[k:k + 1, :] = mx
        eq = s == mx
        if want_rank:
            rank = jnp.where(eq, float(k), rank)
        s = jnp.where(eq, NEG, s)
    return rank


def _route_kernel(x_ref, g_ref, sh_ref, sc_ref, wqh_ref, wql_ref, k1h_ref, k1l_ref, k2h_ref, k2l_ref,
                  ht_ref, e1_ref, l_ref, e2_ref, r2_ref, v1_scr, v2_scr, *, tm):
    h2 = _rms_mod(x_ref[...], g_ref[...], sh_ref[0], sc_ref[0])
    ht = h2.T
    ht_hi, ht_lo = _split_bf16(ht)
    ht_ref[0] = ht_hi
    qt = _dot3(wqh_ref[...], wql_ref[...], ht_hi, ht_lo)
    half = PEER_NKEYS // 2
    sub = lax.broadcasted_iota(jnp.int32, (8, tm), 0)
    for h in range(PEER_HEADS):
        base = h * PEER_NKEYS
        q1h, q1l = _split_bf16(qt[base:base + half])
        q2h, q2l = _split_bf16(qt[base + half:base + PEER_NKEYS])
        s1 = _dot3(k1h_ref[h], k1l_ref[h], q1h, q1l)
        s2 = _dot3(k2h_ref[h], k2l_ref[h], q2h, q2l)
        _extract_top(s1, v1_scr, False)
        r2 = _extract_top(s2, v2_scr, True)
        v1 = v1_scr[...]
        v2 = v2_scr[...]
        ev1 = jnp.exp(v1 - v1[0:1])
        ev2 = jnp.exp(v2 - v2[0:1])
        e1 = jnp.exp(s1 - v1[0:1])
        e2 = jnp.exp(s2 - v2[0:1])
        blocks = [ev1[0:1] * ev2, ev1[1:2] * ev2[0:8]]
        for a in range(2, 8):
            blocks.append(jnp.where(sub < PEER_TOPK // (a + 1), ev1[a:a + 1] * ev2[0:8], -1.0))
        blocks.append(ev1[8:16] * ev2[0:1])
        cand = jnp.concatenate(blocks, axis=0)
        rem = cand
        thr = None
        for _ in range(PEER_TOPK):
            thr = jnp.max(rem, axis=0, keepdims=True)
            rem = jnp.where(rem == thr, -1.0, rem)
        sel = cand >= thr
        z = jnp.sum(jnp.where(sel, cand, 0.0), axis=0, keepdims=True)
        rz = 1.0 / z
        self = jnp.where(sel, 1.0, 0.0)
        counts = [jnp.sum(self[0:16], axis=0, keepdims=True)]
        for a in range(1, 8):
            counts.append(jnp.sum(self[8 + 8 * a:16 + 8 * a], axis=0, keepdims=True))
        for a in range(8, 16):
            counts.append(self[72 + a - 8:73 + a - 8])
        lmap = jnp.zeros((PEER_NKEYS, tm), F32)
        for a in range(PEER_TOPK):
            lmap = jnp.where(s1 == v1[a:a + 1], counts[a], lmap)
        e1_ref[0, base:base + PEER_NKEYS, :] = _twice_bf16(e1 * rz)
        l_ref[0, base:base + PEER_NKEYS, :] = _twice_bf16(lmap)
        e2_ref[0, base:base + PEER_NKEYS, :] = e2.astype(BF16)
        r2_ref[0, base:base + PEER_NKEYS, :] = r2.astype(BF16)


def _route(x, g, sh, sc, lw, per_row, rows_per_batch, tm):
    m = x.shape[0]
    nt = m // tm
    tpb = max(rows_per_batch // tm, 1)
    hk = PEER_HEADS * PEER_NKEYS
    full = lambda a: pl.BlockSpec(a.shape, lambda i: (0,) * a.ndim)
    tile = lambda r: pl.BlockSpec((1, r, tm), lambda i: (i, 0, 0))
    ws = [lw["wq_hi"], lw["wq_lo"], lw["k1_hi"], lw["k1_lo"], lw["k2_hi"], lw["k2_lo"]]
    return pl.pallas_call(
        functools.partial(_route_kernel, tm=tm),
        grid=(nt,),
        in_specs=[pl.BlockSpec((tm, D), lambda i: (i, 0)), full(g),
                  _mod_spec(per_row, tm, tpb, 0), _mod_spec(per_row, tm, tpb, 0)] + [full(a) for a in ws],
        out_specs=[tile(D), tile(hk), tile(hk), tile(hk), tile(hk)],
        out_shape=[jax.ShapeDtypeStruct((nt, D, tm), BF16),
                   jax.ShapeDtypeStruct((nt, hk, tm), F32),
                   jax.ShapeDtypeStruct((nt, hk, tm), F32),
                   jax.ShapeDtypeStruct((nt, hk, tm), BF16),
                   jax.ShapeDtypeStruct((nt, hk, tm), BF16)],
        scratch_shapes=[pltpu.VMEM((PEER_TOPK, tm), F32), pltpu.VMEM((PEER_TOPK, tm), F32)],
        compiler_params=_cparams(("arbitrary",)),
        name="peer_route",
    )(x, g, sh, sc, *ws)


def _gelu_tanh(x):
    return 0.5 * x * (1.0 + jnp.tanh(0.7978845608028654 * (x + 0.044715 * (x * x * x))))


def _experts_kernel(ht_ref, e1_ref, l_ref, e2_ref, r2_ref, u_ref, vt_ref, x_ref, ga_ref, fg_ref,
                    out_ref, acc_ref, hid_ref, e2_scr, r2_scr, a_scr, w_scr, *, tm, te, final):
    e = pl.program_id(1)
    ne = pl.num_programs(1) - 1
    lc = 128
    n_i1 = te // PEER_NKEYS

    def second_matmul():
        acc_ref[...] += jnp.dot(vt_ref[...], hid_ref[(e + 1) % 2], preferred_element_type=F32)

    def first_matmul():
        a_scr[...] = jnp.dot(u_ref[...], ht_ref[0], preferred_element_type=F32)

    def weights():
        rg = 4
        for c in range(tm // lc):
            cs = slice(c * lc, (c + 1) * lc)
            for r0 in range(0, n_i1, rg):
                ws = [None] * rg
                for h in range(PEER_HEADS):
                    start = pl.multiple_of(h * PEER_NKEYS + e * n_i1, n_i1)
                    e1rows = e1_ref[0, pl.ds(start, n_i1), cs]
                    lrows = l_ref[0, pl.ds(start, n_i1), cs]
                    hs = slice(h * PEER_NKEYS, (h + 1) * PEER_NKEYS)
                    e2 = e2_scr[hs, cs]
                    r2 = r2_scr[hs, cs]
                    for k in range(rg):
                        r = r0 + k
                        e1b = pltpu.bitcast(jnp.broadcast_to(e1rows[r:r + 1], (PEER_NKEYS // 2, lc)), BF16)
                        lb = pltpu.bitcast(jnp.broadcast_to(lrows[r:r + 1], (PEER_NKEYS // 2, lc)), BF16)
                        term = jnp.where(r2 < lb, e1b * e2, 0.0)
                        ws[k] = term if ws[k] is None else ws[k] + term
                for k in range(rg):
                    w_scr[(r0 + k) * PEER_NKEYS:(r0 + k + 1) * PEER_NKEYS, cs] = ws[k]

    def gate():
        hid_ref[e % 2] = w_scr[...] * _gelu_tanh(a_scr[...]).astype(BF16)

    @pl.when(e == 0)
    def _():
        acc_ref[...] = jnp.zeros_like(acc_ref)
        e2_scr[...] = e2_ref[0]
        r2_scr[...] = r2_ref[0]
        first_matmul()
        weights()
        gate()

    @pl.when((e > 0) & (e < ne))
    def _():
        first_matmul()
        weights()
        second_matmul()
        gate()

    @pl.when(e == ne)
    def _():
        second_matmul()
        y = x_ref[...] + ga_ref[0] * acc_ref[...].T
        if final:
            ms = jnp.mean(y * y, axis=-1, keepdims=True)
            y = y * lax.rsqrt(ms + RMS_EPS) * fg_ref[...]
        out_ref[...] = y


def _experts(ht, e1, lmap, e2, r2, u_bf, vt_bf, x, ga, fg, per_row, rows_per_batch, tm, te, final):
    m = x.shape[0]
    nt = m // tm
    ne = PEER_EXPERTS // te
    tpb = max(rows_per_batch // tm, 1)
    hk = PEER_HEADS * PEER_NKEYS
    tile = lambda r: pl.BlockSpec((1, r, tm), lambda i, e: (i, 0, 0))
    return pl.pallas_call(
        functools.partial(_experts_kernel, tm=tm, te=te, final=final),
        grid=(nt, ne + 1),
        in_specs=[tile(D), tile(hk), tile(hk), tile(hk), tile(hk),
                  pl.BlockSpec((te, D), lambda i, e: (jnp.minimum(e, ne - 1), 0)),
                  pl.BlockSpec((D, te), lambda i, e: (0, jnp.maximum(e - 1, 0))),
                  pl.BlockSpec((tm, D), lambda i, e: (i, 0)),
                  _mod_spec(per_row, tm, tpb, 0),
                  pl.BlockSpec((1, D), lambda i, e: (0, 0))],
        out_specs=pl.BlockSpec((tm, D), lambda i, e: (i, 0)),
        out_shape=jax.ShapeDtypeStruct((m, D), F32),
        scratch_shapes=[pltpu.VMEM((D, tm), F32), pltpu.VMEM((2, te, tm), BF16),
                        pltpu.VMEM((hk, tm), BF16), pltpu.VMEM((hk, tm), BF16), pltpu.VMEM((te, tm), F32),
                        pltpu.VMEM((te, tm), BF16)],
        compiler_params=_cparams(("arbitrary", "arbitrary")),
        name="peer_experts",
    )(ht, e1, lmap, e2, r2, u_bf, vt_bf, x, ga, fg)


def _group_rows(per_row, vec, rows_per_batch, tm):
    if per_row:
        rows = jnp.repeat(vec, rows_per_batch, axis=0)
        return rows.reshape(rows.shape[0] // tm, tm, D)
    return vec[:, None, :]


def _trunk_layer(x, mods, lw, per_row, b, t, ret_st0, pool_prev, bufs, layer, pos, final, final_g):
    sh1, sc1, ga1, sh2, sc2, ga2 = mods
    m = b * t
    gm = lambda v, tm: _group_rows(per_row, v, t, tm)
    z = _inproj(x, lw["g_mix"], gm(sh1, 512), gm(sc1, 512), lw["w_in"], per_row, t)
    z3 = z.reshape(b, t, IN_WIDTH)

    cos2, sin2 = _rope_tables(pos)
    chunk = RET_CHUNK if t >= RET_CHUNK else t
    yr, new_ret = _retention(z3, ret_st0, cos2, sin2, lw["gn_g"], lw["gn_b"], chunk)

    if per_row:
        p = _pool(z3, pool_prev, t, PAST_LEN, False)
        att_parts = _datt_sample(z3, bufs, layer)
    else:
        p = _pool(z3, None, 512, 0, True)
        att_parts = [_datt_prompt(z3, gi, dil) for gi, (win, dil) in enumerate(ATT_PATTERNS)]

    x1 = _mix(x, gm(ga1, 256), z, yr.reshape(m, 512), p.reshape(m, 512), att_parts,
              lw["w_ret_out"], lw["pool_w"], lw["pool_scale"], lw["w_att_out"], lw["w_out"], per_row, t)

    tm = 512
    ht, e1, lmap, e2, r2 = _route(x1, lw["g_ffn"], gm(sh2, tm), gm(sc2, tm), lw, per_row, t, tm)
    x2 = _experts(ht, e1, lmap, e2, r2, lw["u"], lw["v_t"], x1, gm(ga2, tm), final_g, per_row, t, tm,
                  8 * PEER_NKEYS, final)

    new_kv = []
    for gi, (win, dil) in enumerate(ATT_PATTERNS):
        keep = t if per_row else min(win, t)
        kk = z3[:, t - keep:, (COL_KA + gi) * CB:(COL_KA + gi + 1) * CB].reshape(b, keep, ATT_HEADS, ATT_DH)
        vv = z3[:, t - keep:, (COL_VA + gi) * CB:(COL_VA + gi + 1) * CB].reshape(b, keep, ATT_HEADS, ATT_DH)
        new_kv.append(jnp.stack([kk, vv], axis=2))
    up = z3[:, :, COL_UP * CB:COL_UP * CB + 512]
    if per_row:
        new_pool = jnp.concatenate([pool_prev[:, 1:], up], axis=1)[:, -(POOL_HALO - 1):]
    else:
        new_pool = up[:, t - (POOL_HALO - 1):]
    return x2, (new_ret, new_pool, new_kv)


def kernel(x_prompt, x_sample, state_ret, state_pool, cache_kv_w128, cache_kv_w512, cache_kv_w2048,
           c_prompt, c_sample, ada_w, ada_b, norm_mix_g, norm_ffn_g, w_in, ret_gn_g, ret_gn_b,
           w_ret_out, pool_w, pool_scale, w_att_out, w_out, peer_wq, peer_keys, peer_u, peer_v,
           final_norm_g):
    depth = ada_w.shape[0]
    bp, s, _ = x_prompt.shape
    bs, t, _ = x_sample.shape
    caches_t = tuple(jnp.transpose(c, (0, 1, 3, 4, 5, 2)) for c in (cache_kv_w128, cache_kv_w512, cache_kv_w2048))

    mods_all = _adaln(jnp.concatenate([c_prompt, c_sample], axis=0), ada_w, ada_b)

    xp = x_prompt.reshape(bp * s, D)
    xs = x_sample.reshape(bs * t, D)
    final_g = final_norm_g.reshape(1, D)
    zero_state = jnp.zeros((bp, RET_HEADS, RET_DK, RET_DV), F32)
    outs_p, outs_s = [], []
    for l in range(depth):
        n_gate = 3 * D
        w_in_l = jnp.concatenate([w_in[l][:, IN_WIDTH - n_gate:], w_in[l][:, :IN_WIDTH - n_gate]], axis=1)
        wq_hi, wq_lo = _split_bf16(peer_wq[l].T)
        k1_hi, k1_lo = _split_bf16(peer_keys[l][:, 0])
        k2_hi, k2_lo = _split_bf16(peer_keys[l][:, 1])
        lw = {
            "g_mix": norm_mix_g[l].reshape(1, D), "g_ffn": norm_ffn_g[l].reshape(1, D),
            "w_in": w_in_l.astype(BF16),
            "gn_g": ret_gn_g[l].reshape(1, -1), "gn_b": ret_gn_b[l].reshape(1, -1),
            "w_ret_out": w_ret_out[l].astype(BF16), "pool_w": pool_w[l].astype(BF16),
            "pool_scale": pool_scale[l].reshape(1, D), "w_att_out": w_att_out[l].astype(BF16),
            "w_out": w_out[l].astype(BF16),
            "wq_hi": wq_hi, "wq_lo": wq_lo, "k1_hi": k1_hi, "k1_lo": k1_lo, "k2_hi": k2_hi, "k2_lo": k2_lo,
            "u": peer_u[l].astype(BF16), "v_t": peer_v[l].T.astype(BF16),
        }
        ml = mods_all[l]
        mods_p = tuple(ml[:bp, k * D:(k + 1) * D] for k in range(6))
        mods_s = tuple(ml[bp:, k * D:(k + 1) * D] for k in range(6))
        final = l == depth - 1
        xp, st_p = _trunk_layer(xp, mods_p, lw, False, bp, s, zero_state, None, None, l,
                                jnp.arange(s), final, final_g)
        pool_prev = jnp.pad(state_pool[l], ((0, 0), (1, 0), (0, 0)))
        xs, st_s = _trunk_layer(xs, mods_s, lw, True, bs, t, state_ret[l], pool_prev, caches_t, l,
                                PAST_LEN + jnp.arange(t), final, final_g)
        outs_p.append(st_p)
        outs_s.append(st_s)

    def stack(outs, pick):
        return jnp.stack([pick(o) for o in outs])

    res = [xp.reshape(bp, s, D), xs.reshape(bs, t, D),
           stack(outs_p, lambda o: o[0]), stack(outs_s, lambda o: o[0]),
           stack(outs_p, lambda o: o[1]), stack(outs_s, lambda o: o[1])]
    for gi in range(len(ATT_PATTERNS)):
        res.append(stack(outs_p, lambda o: o[2][gi]))
        res.append(stack(outs_s, lambda o: o[2][gi]))
    return tuple(res)
```

```python
import functools
import math

import jax
import jax.numpy as jnp
from jax import lax
from jax.experimental import pallas as pl
from jax.experimental.pallas import tpu as pltpu

F32 = jnp.float32
BF16 = jnp.bfloat16

D = 1024
RMS_EPS = 1e-6
GN_EPS = 1e-5
PAST_LEN = 2048

RET_HEADS = 4
RET_DK = 64
RET_DV = 128
RET_CHUNK = 128
ROPE_BASE = 10000.0
RET_LOG_GAMMA = tuple(math.log(1.0 - 2.0 ** (-5.0 - h)) for h in range(RET_HEADS))

POOL_WINDOWS = (2, 4, 8, 16)
POOL_GW = 128
POOL_HALO = 16

ATT_PATTERNS = ((128, 1), (512, 4), (2048, 16))
ATT_HEADS = 4
ATT_DH = 64
ATT_BLOCK = 128
ATT_PAIR = 2
ATT_BACK = 128
NEG = -1e30

PEER_HEADS = 8
PEER_NKEYS = 128
PEER_EXPERTS = PEER_NKEYS * PEER_NKEYS
PEER_TOPK = 16
PEER_TILE = 8 * PEER_NKEYS
LANE_PAD = 128

IN_WIDTH = 7424
CB = 256
COL_GATE = 0
COL_QR, COL_KR, COL_VR, COL_GR, COL_UP = 12, 13, 14, 16, 18
COL_QA, COL_KA, COL_VA = 20, 23, 26
N_CB = IN_WIDTH // CB

VMEM_LIMIT = 60 * 1024 * 1024


def _cparams(sem):
    return pltpu.CompilerParams(dimension_semantics=sem, vmem_limit_bytes=VMEM_LIMIT)


def _mod_spec(per_row, tm, tiles_per_batch, grid_pos):
    if per_row:
        return pl.BlockSpec((1, tm, D), lambda *g: (g[grid_pos], 0, 0))
    return pl.BlockSpec((1, 1, D), lambda *g: (g[grid_pos] // tiles_per_batch, 0, 0))


def _rms_mod(x, g, sh, sc):
    ms = jnp.mean(x * x, axis=-1, keepdims=True)
    h = x * lax.rsqrt(ms + RMS_EPS) * g
    return h * (1.0 + sc) + sh


def _ada_kernel(c_ref, w_ref, b_ref, o_ref):
    c = c_ref[...]
    s = c * jax.nn.sigmoid(c)
    o_ref[0] = jnp.dot(s.astype(BF16), w_ref[0].astype(BF16), preferred_element_type=F32) + b_ref[0]


def _adaln(c_all, ada_w, ada_b):
    depth, _, n = ada_w.shape
    m = c_all.shape[0]
    tn = 512
    return pl.pallas_call(
        _ada_kernel,
        grid=(depth, n // tn),
        in_specs=[pl.BlockSpec((m, D), lambda l, j: (0, 0)),
                  pl.BlockSpec((1, D, tn), lambda l, j: (l, 0, j)),
                  pl.BlockSpec((1, 1, tn), lambda l, j: (l, 0, j))],
        out_specs=pl.BlockSpec((1, m, tn), lambda l, j: (l, 0, j)),
        out_shape=jax.ShapeDtypeStruct((depth, m, n), F32),
        compiler_params=_cparams(("arbitrary", "arbitrary")),
        name="adaln",
    )(c_all, ada_w, ada_b.reshape(depth, 1, n))


def _inproj_kernel(x_ref, g_ref, sh_ref, sc_ref, w_ref, o_ref):
    h = _rms_mod(x_ref[...], g_ref[...], sh_ref[0], sc_ref[0])
    o_ref[...] = jnp.dot(h.astype(BF16), w_ref[...], preferred_element_type=F32)


def _inproj(x, g, sh, sc, w_bf, per_row, rows_per_batch):
    m = x.shape[0]
    tm = 512
    tn = IN_WIDTH // 2
    tpb = max(rows_per_batch // tm, 1)
    return pl.pallas_call(
        _inproj_kernel,
        grid=(2, m // tm),
        in_specs=[pl.BlockSpec((tm, D), lambda j, i: (i, 0)),
                  pl.BlockSpec((1, D), lambda j, i: (0, 0)),
                  _mod_spec(per_row, tm, tpb, 1),
                  _mod_spec(per_row, tm, tpb, 1),
                  pl.BlockSpec((D, tn), lambda j, i: (0, j))],
        out_specs=pl.BlockSpec((tm, tn), lambda j, i: (i, j)),
        out_shape=jax.ShapeDtypeStruct((m, IN_WIDTH), F32),
        compiler_params=_cparams(("arbitrary", "arbitrary")),
        name="inproj",
    )(x, g, sh, sc, w_bf)


def _ret_kernel(q_ref, k_ref, v_ref, g_ref, cos_ref, sin_ref, st0_ref, gng_ref, gnb_ref,
                y_ref, st_ref, st_scr, *, C):
    c = pl.program_id(1)

    @pl.when(c == 0)
    def _():
        st_scr[...] = st0_ref[0]

    cos = cos_ref[...]
    sin = sin_ref[...]
    lane = lax.broadcasted_iota(jnp.int32, (C, RET_HEADS * RET_DK), 1)
    first_half = (lane % RET_DK) < (RET_DK // 2)

    def rot(x):
        nl = RET_HEADS * RET_DK
        swapped = jnp.where(first_half, pltpu.roll(x, nl - RET_DK // 2, 1), pltpu.roll(x, RET_DK // 2, 1))
        return x * cos + swapped * sin

    q = rot(q_ref[0])
    k = rot(k_ref[0]) * (RET_DK ** -0.5)
    v = v_ref[0]
    g = g_ref[0]
    ii = lax.broadcasted_iota(jnp.int32, (C, C), 0)
    jj = lax.broadcasted_iota(jnp.int32, (C, C), 1)
    diff = (ii - jj).astype(F32)
    causal = ii >= jj
    row = lax.broadcasted_iota(jnp.int32, (C, 1), 0).astype(F32)
    for h in range(RET_HEADS):
        lg = RET_LOG_GAMMA[h]
        qh = q[:, h * RET_DK:(h + 1) * RET_DK].astype(BF16)
        khf = k[:, h * RET_DK:(h + 1) * RET_DK]
        vh = v[:, h * RET_DV:(h + 1) * RET_DV].astype(BF16)
        st = st_scr[h]
        decay = jnp.where(causal, jnp.exp(lg * jnp.maximum(diff, 0.0)), 0.0)
        s = lax.dot_general(qh, khf.astype(BF16), (((1,), (1,)), ((), ())), preferred_element_type=F32) * decay
        o = jnp.dot(s.astype(BF16), vh, preferred_element_type=F32)
        o = o + jnp.dot(qh, st.astype(BF16), preferred_element_type=F32) * jnp.exp(lg * (row + 1.0))
        kdec = (khf * jnp.exp(lg * (C - 1.0 - row))).astype(BF16)
        st_scr[h] = math.exp(lg * C) * st + lax.dot_general(
            kdec, vh, (((0,), (0,)), ((), ())), preferred_element_type=F32)
        mu = jnp.mean(o, axis=-1, keepdims=True)
        var = jnp.mean(jnp.square(o - mu), axis=-1, keepdims=True)
        on = (o - mu) * lax.rsqrt(var + GN_EPS)
        on = on * gng_ref[:, h * RET_DV:(h + 1) * RET_DV] + gnb_ref[:, h * RET_DV:(h + 1) * RET_DV]
        gh = g[:, h * RET_DV:(h + 1) * RET_DV]
        y_ref[0, :, h * RET_DV:(h + 1) * RET_DV] = gh * jax.nn.sigmoid(gh) * on

    @pl.when(c == pl.num_programs(1) - 1)
    def _():
        st_ref[0] = st_scr[...]


def _retention(z3, st0, cos2, sin2, gn_g, gn_b, C):
    b, t, _ = z3.shape
    nc = t // C
    vq = RET_HEADS * RET_DK
    vv = RET_HEADS * RET_DV
    return pl.pallas_call(
        functools.partial(_ret_kernel, C=C),
        grid=(b, nc),
        in_specs=[pl.BlockSpec((1, C, vq), lambda i, c: (i, c, COL_QR)),
                  pl.BlockSpec((1, C, vq), lambda i, c: (i, c, COL_KR)),
                  pl.BlockSpec((1, C, vv), lambda i, c: (i, c, COL_VR // 2)),
                  pl.BlockSpec((1, C, vv), lambda i, c: (i, c, COL_GR // 2)),
                  pl.BlockSpec((C, vq), lambda i, c: (c, 0)),
                  pl.BlockSpec((C, vq), lambda i, c: (c, 0)),
                  pl.BlockSpec((1, RET_HEADS, RET_DK, RET_DV), lambda i, c: (i, 0, 0, 0)),
                  pl.BlockSpec((1, vv), lambda i, c: (0, 0)),
                  pl.BlockSpec((1, vv), lambda i, c: (0, 0))],
        out_specs=[pl.BlockSpec((1, C, vv), lambda i, c: (i, c, 0)),
                   pl.BlockSpec((1, RET_HEADS, RET_DK, RET_DV), lambda i, c: (i, 0, 0, 0))],
        out_shape=[jax.ShapeDtypeStruct((b, t, vv), F32),
                   jax.ShapeDtypeStruct((b, RET_HEADS, RET_DK, RET_DV), F32)],
        scratch_shapes=[pltpu.VMEM((RET_HEADS, RET_DK, RET_DV), F32)],
        compiler_params=_cparams(("arbitrary", "arbitrary")),
        name="retention",
    )(z3, z3, z3, z3, cos2, sin2, st0, gn_g, gn_b)


def _rope_tables(pos):
    half = RET_DK // 2
    inv = 1.0 / (ROPE_BASE ** jnp.linspace(0.0, 1.0, half, dtype=F32))
    ang = pos.astype(F32)[:, None] * inv[None, :]
    cos, sin = jnp.cos(ang), jnp.sin(ang)
    cos2 = jnp.tile(jnp.concatenate([cos, cos], axis=-1), (1, RET_HEADS))
    sin2 = jnp.tile(jnp.concatenate([-sin, sin], axis=-1), (1, RET_HEADS))
    return cos2, sin2


def _pool_kernel(prev_ref, cur_ref, p_ref, *, tm, pos0, zero_first):
    i = pl.program_id(1)
    cur = cur_ref[0]
    prev = prev_ref[0]
    if zero_first:
        prev = jnp.where(i == 0, 0.0, prev)
    ext = jnp.concatenate([prev, cur], axis=0)
    row = lax.broadcasted_iota(jnp.int32, (tm, 1), 0) + i * tm + pos0
    acc = ext
    sums = []
    for s in (1, 2, 4, 8):
        acc = acc + pltpu.roll(acc, s, 0)
        sums.append(acc)
    for gi, w in enumerate(POOL_WINDOWS):
        sl = slice(gi * POOL_GW, (gi + 1) * POOL_GW)
        win = sums[gi][POOL_HALO:, sl]
        cnt = jnp.minimum(row + 1, w).astype(F32)
        p_ref[0, :, sl] = win / cnt - cur[:, sl]


def _pool(z3, prev, tm, pos0, zero_first):
    b, t, _ = z3.shape
    nt = t // tm
    width = 4 * POOL_GW
    if prev is None:
        prev_arr = z3
        hb = tm // POOL_HALO
        prev_spec = pl.BlockSpec((1, POOL_HALO, width), lambda bi, i: (bi, jnp.maximum(i * hb - 1, 0), COL_UP // 2))
    else:
        prev_arr = prev
        prev_spec = pl.BlockSpec((1, POOL_HALO, width), lambda bi, i: (bi, 0, 0))
    return pl.pallas_call(
        functools.partial(_pool_kernel, tm=tm, pos0=pos0, zero_first=zero_first),
        grid=(b, nt),
        in_specs=[prev_spec, pl.BlockSpec((1, tm, width), lambda bi, i: (bi, i, COL_UP // 2))],
        out_specs=pl.BlockSpec((1, tm, width), lambda bi, i: (bi, i, 0)),
        out_shape=jax.ShapeDtypeStruct((b, t, width), F32),
        compiler_params=_cparams(("arbitrary", "arbitrary")),
        name="pool",
    )(prev_arr, z3)


def _datt_kernel(q_ref, kc_ref, kp_ref, vc_ref, vp_ref, o_ref, l_ref, *, dil):
    n = pl.program_id(1)
    qb = ATT_BLOCK
    ii = lax.broadcasted_iota(jnp.int32, (qb, qb), 0)
    jj = lax.broadcasted_iota(jnp.int32, (qb, qb), 1)
    mask_c = jj <= ii
    mask_p = (jj >= ii) & (n > 0)
    nt = (((1,), (1,)), ((), ()))

    def residue(r, carry):
        rows = pl.ds(r, qb, stride=dil) if dil > 1 else pl.ds(0, qb)
        q = q_ref[0, rows, :] * (ATT_DH ** -0.5)
        kc, kp, vc, vp = kc_ref[0, rows, :], kp_ref[0, rows, :], vc_ref[0, rows, :], vp_ref[0, rows, :]
        outs, lses = [], []
        for h in range(ATT_PAIR):
            sl = slice(h * ATT_DH, (h + 1) * ATT_DH)
            qh = q[:, sl].astype(BF16)
            s_c = jnp.where(mask_c, lax.dot_general(qh, kc[:, sl].astype(BF16), nt, preferred_element_type=F32), NEG)
            s_p = jnp.where(mask_p, lax.dot_general(qh, kp[:, sl].astype(BF16), nt, preferred_element_type=F32), NEG)
            m = jnp.maximum(jnp.max(s_c, axis=-1, keepdims=True), jnp.max(s_p, axis=-1, keepdims=True))
            p_c = jnp.exp(s_c - m)
            p_p = jnp.exp(s_p - m)
            den = jnp.sum(p_c, axis=-1, keepdims=True) + jnp.sum(p_p, axis=-1, keepdims=True)
            num = (jnp.dot(p_c.astype(BF16), vc[:, sl].astype(BF16), preferred_element_type=F32)
                   + jnp.dot(p_p.astype(BF16), vp[:, sl].astype(BF16), preferred_element_type=F32))
            outs.append(num / den)
            lses.append(jnp.broadcast_to(m + jnp.log(den), (qb, ATT_DH)))
        o_ref[0, rows, :] = jnp.concatenate(outs, axis=-1)
        l_ref[0, rows, :] = jnp.concatenate(lses, axis=-1)
        return carry

    if dil == 1:
        residue(0, 0)
    else:
        lax.fori_loop(0, dil, residue, 0)


def _datt_prompt(z3, gi, dil):
    b, s, _ = z3.shape
    tb = ATT_BLOCK * dil
    w = ATT_HEADS * ATT_DH
    wp = ATT_PAIR * ATT_DH
    npair = ATT_HEADS // ATT_PAIR

    def spec(col, prev):
        if prev:
            return pl.BlockSpec((1, tb, wp), lambda bi, n, hp: (bi, jnp.maximum(n - 1, 0), (col + gi) * npair + hp))
        return pl.BlockSpec((1, tb, wp), lambda bi, n, hp: (bi, n, (col + gi) * npair + hp))

    o, lse = pl.pallas_call(
        functools.partial(_datt_kernel, dil=dil),
        grid=(b, s // tb, npair),
        in_specs=[spec(COL_QA, False), spec(COL_KA, False), spec(COL_KA, True),
                  spec(COL_VA, False), spec(COL_VA, True)],
        out_specs=[pl.BlockSpec((1, tb, wp), lambda bi, n, hp: (bi, n, hp))] * 2,
        out_shape=[jax.ShapeDtypeStruct((b, s, w), F32)] * 2,
        compiler_params=_cparams(("arbitrary", "arbitrary", "arbitrary")),
        name=f"datt_prompt_d{dil}",
    )(z3, z3, z3, z3, z3)
    return o.reshape(b * s, w), lse.reshape(b * s, w)


def _satt_kernel(*refs, T):
    ng = len(ATT_PATTERNS)
    qkv = refs[:3 * ng]
    bufs = refs[3 * ng:4 * ng]
    outs = refs[4 * ng:]
    nt = (((1,), (1,)), ((), ()))
    for gi, (win, dil) in enumerate(ATT_PATTERNS):
        q = qkv[gi][0] * (ATT_DH ** -0.5)
        kn = qkv[ng + gi][0]
        vn = qkv[2 * ng + gi][0]
        buf = bufs[gi]
        Wb = buf.shape[-1]
        qi_b = lax.broadcasted_iota(jnp.int32, (T, Wb), 0)
        dist_b = Wb + qi_b - lax.broadcasted_iota(jnp.int32, (T, Wb), 1)
        ok_b = (dist_b % dil == 0) & (dist_b <= dil * ATT_BACK)
        dist_n = lax.broadcasted_iota(jnp.int32, (T, T), 0) - lax.broadcasted_iota(jnp.int32, (T, T), 1)
        ok_n = (dist_n >= 0) & (dist_n % dil == 0)
        o_parts, l_parts = [], []
        for h in range(ATT_HEADS):
            sl = slice(h * ATT_DH, (h + 1) * ATT_DH)
            qh = q[:, sl].astype(BF16)
            kt = buf[0, 0, h].astype(BF16)
            vt = buf[0, 1, h].astype(BF16)
            s_b = jnp.where(ok_b, jnp.dot(qh, kt, preferred_element_type=F32), NEG)
            s_n = jnp.where(ok_n, lax.dot_general(qh, kn[:, sl].astype(BF16), nt, preferred_element_type=F32), NEG)
            m = jnp.maximum(jnp.max(s_b, axis=-1, keepdims=True), jnp.max(s_n, axis=-1, keepdims=True))
            p_b = jnp.exp(s_b - m)
            p_n = jnp.exp(s_n - m)
            den = jnp.sum(p_b, axis=-1, keepdims=True) + jnp.sum(p_n, axis=-1, keepdims=True)
            num = (lax.dot_general(p_b.astype(BF16), vt, nt, preferred_element_type=F32)
                   + jnp.dot(p_n.astype(BF16), vn[:, sl].astype(BF16), preferred_element_type=F32))
            o_parts.append(num / den)
            l_parts.append(jnp.broadcast_to(m + jnp.log(den), (T, ATT_DH)))
        outs[2 * gi][0] = jnp.concatenate(o_parts, axis=-1)
        outs[2 * gi + 1][0] = jnp.concatenate(l_parts, axis=-1)


def _datt_sample(z3, bufs_t, layer):
    b, t, _ = z3.shape
    w = ATT_HEADS * ATT_DH
    ng = len(ATT_PATTERNS)
    col = lambda c: pl.BlockSpec((1, t, w), lambda i: (i, 0, c))
    in_specs = ([col(COL_QA + gi) for gi in range(ng)] + [col(COL_KA + gi) for gi in range(ng)]
                + [col(COL_VA + gi) for gi in range(ng)]
                + [pl.BlockSpec((None, 1) + bt.shape[2:], lambda i: (layer, i, 0, 0, 0, 0)) for bt in bufs_t])
    res = pl.pallas_call(
        functools.partial(_satt_kernel, T=t),
        grid=(b,),
        in_specs=in_specs,
        out_specs=[pl.BlockSpec((1, t, w), lambda i: (i, 0, 0))] * (2 * ng),
        out_shape=[jax.ShapeDtypeStruct((b, t, w), F32)] * (2 * ng),
        compiler_params=_cparams(("arbitrary",)),
        name="datt_sample",
    )(*([z3] * (3 * ng)), *bufs_t)
    return [(res[2 * gi].reshape(b * t, w), res[2 * gi + 1].reshape(b * t, w)) for gi in range(ng)]


def _mix_kernel(x_ref, ga_ref, yr_ref, p_ref, o1_ref, l1_ref, o2_ref, l2_ref, o3_ref, l3_ref,
                gr_ref, gp_ref, gatt_ref, wr_ref, wp_ref, ps_ref, wa_ref, wo_ref, out_ref):
    y_r = jnp.dot(yr_ref[...].astype(BF16), wr_ref[...], preferred_element_type=F32)
    p = p_ref[...]
    parts = []
    for gi in range(len(POOL_WINDOWS)):
        parts.append(jnp.dot(p[:, gi * POOL_GW:(gi + 1) * POOL_GW].astype(BF16), wp_ref[gi],
                             preferred_element_type=F32))
    y_p = jnp.concatenate(parts, axis=-1) * ps_ref[...]
    l1, l2, l3 = l1_ref[...], l2_ref[...], l3_ref[...]
    lm = jnp.maximum(jnp.maximum(l1, l2), l3)
    w1, w2, w3 = jnp.exp(l1 - lm), jnp.exp(l2 - lm), jnp.exp(l3 - lm)
    att = (w1 * o1_ref[...] + w2 * o2_ref[...] + w3 * o3_ref[...]) / (w1 + w2 + w3)
    y_a = jnp.dot(att.astype(BF16), wa_ref[...], preferred_element_type=F32)
    mix = (jax.nn.sigmoid(gr_ref[...]) * y_r + jax.nn.sigmoid(gp_ref[...]) * y_p
           + jax.nn.sigmoid(gatt_ref[...]) * y_a)
    out = jnp.dot(mix.astype(BF16), wo_ref[...], preferred_element_type=F32)
    out_ref[...] = x_ref[...] + ga_ref[0] * out


def _mix(x, ga, z, yr, p, att_parts, w_ret_out, pool_w, pool_scale, w_att_out, w_out, per_row, rows_per_batch):
    m = x.shape[0]
    tm = 256
    tpb = max(rows_per_batch // tm, 1)
    row = lambda width: pl.BlockSpec((tm, width), lambda i: (i, 0))
    full = lambda shape: pl.BlockSpec(shape, lambda i: (0,) * len(shape))
    gate = lambda k: pl.BlockSpec((tm, D), lambda i: (i, k))
    (o1, l1), (o2, l2), (o3, l3) = att_parts
    return pl.pallas_call(
        _mix_kernel,
        grid=(m // tm,),
        in_specs=[row(D), _mod_spec(per_row, tm, tpb, 0), row(512), row(512),
                  row(256), row(256), row(256), row(256), row(256), row(256),
                  gate(0), gate(1), gate(2),
                  full((512, D)), full((4, POOL_GW, 256)), full((1, D)), full((256, D)), full((D, D))],
        out_specs=row(D),
        out_shape=jax.ShapeDtypeStruct((m, D), F32),
        compiler_params=_cparams(("arbitrary",)),
        name="mix",
    )(x, ga, yr, p, o1, l1, o2, l2, o3, l3, z, z, z, w_ret_out, pool_w, pool_scale, w_att_out, w_out)


def _split_bf16(x):
    hi = x.astype(BF16)
    return hi, (x - hi.astype(F32)).astype(BF16)


def _dot3(a_hi, a_lo, b_hi, b_lo):
    d = functools.partial(jnp.dot, preferred_element_type=F32)
    return d(a_hi, b_hi) + (d(a_hi, b_lo) + d(a_lo, b_hi))


def _sorting_network(n):
    pairs, p = [], 1
    while p < n:
        k = p
        while k >= 1:
            for j in range(k % p, n - k, 2 * k):
                for i in range(min(k, n - j - k)):
                    if (i + j) // (2 * p) == (i + j + k) // (2 * p):
                        pairs.append((i + j, i + j + k))
            k //= 2
        p *= 2
    return tuple(pairs)


_SORT16 = _sorting_network(PEER_TOPK)


def _extract_top(s, v_scr, tm):
    nt8 = PEER_NKEYS // 8
    for c in range(tm // 128):
        cs = slice(c * 128, (c + 1) * 128)
        a = [s[8 * j:8 * (j + 1), cs] for j in range(nt8)]
        for i, j in _SORT16:
            a[i], a[j] = jnp.maximum(a[i], a[j]), jnp.minimum(a[i], a[j])
        for k in range(PEER_TOPK):
            mx = jnp.max(a[0], axis=0, keepdims=True)
            v_scr[k:k + 1, cs] = mx
            pop = a[0] == mx
            last = PEER_TOPK - 1 - k
            for j in range(last):
                a[j] = jnp.where(pop, a[j + 1], a[j])
            a[last] = jnp.where(pop, NEG, a[last])


def _route_kernel(x_ref, g_ref, sh_ref, sc_ref, wqh_ref, wql_ref, k1h_ref, k1l_ref, k2h_ref, k2l_ref,
                  ht_ref, e1_ref, th_ref, e2_ref, v1_scr, v2_scr, *, tm):
    h2 = _rms_mod(x_ref[...], g_ref[...], sh_ref[0], sc_ref[0])
    ht = h2.T
    ht_hi, ht_lo = _split_bf16(ht)
    ht_ref[0] = ht_hi
    qt = _dot3(wqh_ref[...], wql_ref[...], ht_hi, ht_lo)
    half = PEER_NKEYS // 2
    sub = lax.broadcasted_iota(jnp.int32, (8, tm), 0)
    none = 2.0
    for h in range(PEER_HEADS):
        base = h * PEER_NKEYS
        q1h, q1l = _split_bf16(qt[base:base + half])
        q2h, q2l = _split_bf16(qt[base + half:base + PEER_NKEYS])
        s1 = _dot3(k1h_ref[h], k1l_ref[h], q1h, q1l)
        s2 = _dot3(k2h_ref[h], k2l_ref[h], q2h, q2l)
        e1 = jnp.exp(s1 - jnp.max(s1, axis=0, keepdims=True))
        e2 = jnp.exp(s2 - jnp.max(s2, axis=0, keepdims=True))
        _extract_top(e1, v1_scr, tm)
        _extract_top(e2, v2_scr, tm)
        ev1 = v1_scr[...]
        ev2 = v2_scr[...]
        blocks = [ev1[0:1] * ev2, ev1[1:2] * ev2[0:8]]
        for a in range(2, 8):
            blocks.append(jnp.where(sub < PEER_TOPK // (a + 1), ev1[a:a + 1] * ev2[0:8], -1.0))
        blocks.append(ev1[8:16] * ev2[0:1])
        cand = jnp.concatenate(blocks, axis=0)
        rem = cand
        thr = None
        for _ in range(PEER_TOPK):
            thr = jnp.max(rem, axis=0, keepdims=True)
            rem = jnp.where(rem == thr, -1.0, rem)
        sel = cand >= thr
        z = jnp.sum(jnp.where(sel, cand, 0.0), axis=0, keepdims=True)
        rz = 1.0 / z
        low = jnp.where(sel, jnp.concatenate([ev2] + [ev2[0:8]] * 7 + [jnp.broadcast_to(ev2[0:1], (8, tm))], axis=0),
                        none)
        th_a = [jnp.min(low[0:16], axis=0, keepdims=True)]
        for a in range(1, 8):
            th_a.append(jnp.min(low[8 + 8 * a:16 + 8 * a], axis=0, keepdims=True))
        for a in range(8, 16):
            th_a.append(low[72 + a - 8:73 + a - 8])
        thmap = jnp.full((PEER_NKEYS, tm), none, F32)
        for a in range(PEER_TOPK):
            thmap = jnp.where(e1 == ev1[a:a + 1], th_a[a], thmap)
        rows = slice(base, base + PEER_NKEYS)
        for ref, val in ((e1_ref, e1 * rz), (th_ref, thmap), (e2_ref, e2)):
            ref[0, rows, 0:tm] = val
            ref[0, rows, tm:] = jnp.zeros((PEER_NKEYS, LANE_PAD), F32)


def _route(x, g, sh, sc, lw, per_row, rows_per_batch, tm):
    m = x.shape[0]
    nt = m // tm
    tpb = max(rows_per_batch // tm, 1)
    hk = PEER_HEADS * PEER_NKEYS
    full = lambda a: pl.BlockSpec(a.shape, lambda i: (0,) * a.ndim)
    tile = lambda r: pl.BlockSpec((1, r, tm), lambda i: (i, 0, 0))
    ws = [lw["wq_hi"], lw["wq_lo"], lw["k1_hi"], lw["k1_lo"], lw["k2_hi"], lw["k2_lo"]]
    return pl.pallas_call(
        functools.partial(_route_kernel, tm=tm),
        grid=(nt,),
        in_specs=[pl.BlockSpec((tm, D), lambda i: (i, 0)), full(g),
                  _mod_spec(per_row, tm, tpb, 0), _mod_spec(per_row, tm, tpb, 0)] + [full(a) for a in ws],
        out_specs=[tile(D)] + [pl.BlockSpec((1, hk, tm + LANE_PAD), lambda i: (i, 0, 0))] * 3,
        out_shape=[jax.ShapeDtypeStruct((nt, D, tm), BF16)]
        + [jax.ShapeDtypeStruct((nt, hk, tm + LANE_PAD), F32)] * 3,
        scratch_shapes=[pltpu.VMEM((PEER_TOPK, tm), F32), pltpu.VMEM((PEER_TOPK, tm), F32)],
        compiler_params=_cparams(("arbitrary",)),
        name="peer_route",
    )(x, g, sh, sc, *ws)


def _gelu_tanh(x):
    return 0.5 * x * (1.0 + jnp.tanh(0.7978845608028654 * (x + 0.044715 * (x * x * x))))


def _experts_kernel(ht_ref, e1_ref, th_ref, e2_ref, u_ref, vt_ref, x_ref, ga_ref, fg_ref,
                    out_ref, acc_ref, h_ref, *, tm, final):
    e = pl.program_id(1)
    te = PEER_TILE
    lc = 128
    n_i1 = te // PEER_NKEYS

    @pl.when(e == 0)
    def _():
        acc_ref[...] = jnp.zeros_like(acc_ref)

    a_t = jnp.dot(u_ref[...], ht_ref[0], preferred_element_type=F32)
    for r in range(n_i1):
        for c in range(tm // lc):
            cs = slice(c * lc, (c + 1) * lc)
            w = None
            for h in range(PEER_HEADS):
                start = pl.multiple_of(h * PEER_NKEYS + e * n_i1, n_i1)
                e1row = e1_ref[0, pl.ds(start, n_i1), cs][r:r + 1]
                throw = th_ref[0, pl.ds(start, n_i1), cs][r:r + 1]
                e2 = e2_ref[0, h * PEER_NKEYS:(h + 1) * PEER_NKEYS, cs]
                term = jnp.where(e2 >= throw, e1row * e2, 0.0)
                w = term if w is None else w + term
            rs = slice(r * PEER_NKEYS, (r + 1) * PEER_NKEYS)
            h_ref[rs, cs] = (w * _gelu_tanh(a_t[rs, cs])).astype(BF16)
    acc_ref[...] += jnp.dot(vt_ref[0], h_ref[...], preferred_element_type=F32)

    @pl.when(e == pl.num_programs(1) - 1)
    def _():
        y = x_ref[...] + ga_ref[0] * acc_ref[...].T
        if final:
            ms = jnp.mean(y * y, axis=-1, keepdims=True)
            y = y * lax.rsqrt(ms + RMS_EPS) * fg_ref[...]
        out_ref[...] = y


def _experts(ht, e1, th, e2, u_bf, vt_bf, x, ga, fg, per_row, rows_per_batch, tm, final):
    m = x.shape[0]
    nt = m // tm
    te = PEER_TILE
    ne = PEER_EXPERTS // te
    tpb = max(rows_per_batch // tm, 1)
    hk = PEER_HEADS * PEER_NKEYS
    sel = pl.BlockSpec((1, hk, tm + LANE_PAD), lambda i, e: (i, 0, 0))
    return pl.pallas_call(
        functools.partial(_experts_kernel, tm=tm, final=final),
        grid=(nt, ne),
        in_specs=[pl.BlockSpec((1, D, tm), lambda i, e: (i, 0, 0)), sel, sel, sel,
                  pl.BlockSpec((te, D), lambda i, e: (e, 0)),
                  pl.BlockSpec((1, D, te), lambda i, e: (e, 0, 0)),
                  pl.BlockSpec((tm, D), lambda i, e: (i, 0)),
                  _mod_spec(per_row, tm, tpb, 0),
                  pl.BlockSpec((1, D), lambda i, e: (0, 0))],
        out_specs=pl.BlockSpec((tm, D), lambda i, e: (i, 0)),
        out_shape=jax.ShapeDtypeStruct((m, D), F32),
        scratch_shapes=[pltpu.VMEM((D, tm), F32), pltpu.VMEM((te, tm), BF16)],
        compiler_params=_cparams(("arbitrary", "arbitrary")),
        name="peer_experts",
    )(ht, e1, th, e2, u_bf, vt_bf, x, ga, fg)


def _group_rows(per_row, vec, rows_per_batch, tm):
    if per_row:
        rows = jnp.repeat(vec, rows_per_batch, axis=0)
        return rows.reshape(rows.shape[0] // tm, tm, D)
    return vec[:, None, :]


def _trunk_layer(x, mods, lw, per_row, b, t, ret_st0, pool_prev, bufs, layer, pos, final, final_g):
    sh1, sc1, ga1, sh2, sc2, ga2 = mods
    m = b * t
    gm = lambda v, tm: _group_rows(per_row, v, t, tm)
    z = _inproj(x, lw["g_mix"], gm(sh1, 512), gm(sc1, 512), lw["w_in"], per_row, t)
    z3 = z.reshape(b, t, IN_WIDTH)

    cos2, sin2 = _rope_tables(pos)
    chunk = RET_CHUNK if t >= RET_CHUNK else t
    yr, new_ret = _retention(z3, ret_st0, cos2, sin2, lw["gn_g"], lw["gn_b"], chunk)

    if per_row:
        p = _pool(z3, pool_prev, t, PAST_LEN, False)
        att_parts = _datt_sample(z3, bufs, layer)
    else:
        p = _pool(z3, None, 512, 0, True)
        att_parts = [_datt_prompt(z3, gi, dil) for gi, (win, dil) in enumerate(ATT_PATTERNS)]

    x1 = _mix(x, gm(ga1, 256), z, yr.reshape(m, 512), p.reshape(m, 512), att_parts,
              lw["w_ret_out"], lw["pool_w"], lw["pool_scale"], lw["w_att_out"], lw["w_out"], per_row, t)

    tm = 512
    ht, e1, th, e2 = _route(x1, lw["g_ffn"], gm(sh2, tm), gm(sc2, tm), lw, per_row, t, tm)
    x2 = _experts(ht, e1, th, e2, lw["u"], lw["v_t"], x1, gm(ga2, tm), final_g, per_row, t, tm, final)

    new_kv = []
    for gi, (win, dil) in enumerate(ATT_PATTERNS):
        keep = t if per_row else min(win, t)
        kk = z3[:, t - keep:, (COL_KA + gi) * CB:(COL_KA + gi + 1) * CB].reshape(b, keep, ATT_HEADS, ATT_DH)
        vv = z3[:, t - keep:, (COL_VA + gi) * CB:(COL_VA + gi + 1) * CB].reshape(b, keep, ATT_HEADS, ATT_DH)
        new_kv.append(jnp.stack([kk, vv], axis=2))
    up = z3[:, :, COL_UP * CB:COL_UP * CB + 512]
    if per_row:
        new_pool = jnp.concatenate([pool_prev[:, 1:], up], axis=1)[:, -(POOL_HALO - 1):]
    else:
        new_pool = up[:, t - (POOL_HALO - 1):]
    return x2, (new_ret, new_pool, new_kv)


def kernel(x_prompt, x_sample, state_ret, state_pool, cache_kv_w128, cache_kv_w512, cache_kv_w2048,
           c_prompt, c_sample, ada_w, ada_b, norm_mix_g, norm_ffn_g, w_in, ret_gn_g, ret_gn_b,
           w_ret_out, pool_w, pool_scale, w_att_out, w_out, peer_wq, peer_keys, peer_u, peer_v,
           final_norm_g):
    depth = ada_w.shape[0]
    bp, s, _ = x_prompt.shape
    bs, t, _ = x_sample.shape
    caches_t = tuple(jnp.transpose(c, (0, 1, 3, 4, 5, 2)) for c in (cache_kv_w128, cache_kv_w512, cache_kv_w2048))

    mods_all = _adaln(jnp.concatenate([c_prompt, c_sample], axis=0), ada_w, ada_b)

    xp = x_prompt.reshape(bp * s, D)
    xs = x_sample.reshape(bs * t, D)
    final_g = final_norm_g.reshape(1, D)
    zero_state = jnp.zeros((bp, RET_HEADS, RET_DK, RET_DV), F32)
    outs_p, outs_s = [], []
    for l in range(depth):
        n_gate = 3 * D
        w_in_l = jnp.concatenate([w_in[l][:, IN_WIDTH - n_gate:], w_in[l][:, :IN_WIDTH - n_gate]], axis=1)
        wq_hi, wq_lo = _split_bf16(peer_wq[l].T)
        k1_hi, k1_lo = _split_bf16(peer_keys[l][:, 0])
        k2_hi, k2_lo = _split_bf16(peer_keys[l][:, 1])
        lw = {
            "g_mix": norm_mix_g[l].reshape(1, D), "g_ffn": norm_ffn_g[l].reshape(1, D),
            "w_in": w_in_l.astype(BF16),
            "gn_g": ret_gn_g[l].reshape(1, -1), "gn_b": ret_gn_b[l].reshape(1, -1),
            "w_ret_out": w_ret_out[l].astype(BF16), "pool_w": pool_w[l].astype(BF16),
            "pool_scale": pool_scale[l].reshape(1, D), "w_att_out": w_att_out[l].astype(BF16),
            "w_out": w_out[l].astype(BF16),
            "wq_hi": wq_hi, "wq_lo": wq_lo, "k1_hi": k1_hi, "k1_lo": k1_lo, "k2_hi": k2_hi, "k2_lo": k2_lo,
            "u": peer_u[l].astype(BF16),
            "v_t": peer_v[l].reshape(-1, PEER_TILE, D).transpose(0, 2, 1).astype(BF16),
        }
        ml = mods_all[l]
        mods_p = tuple(ml[:bp, k * D:(k + 1) * D] for k in range(6))
        mods_s = tuple(ml[bp:, k * D:(k + 1) * D] for k in range(6))
        final = l == depth - 1
        xp, st_p = _trunk_layer(xp, mods_p, lw, False, bp, s, zero_state, None, None, l,
                                jnp.arange(s), final, final_g)
        pool_prev = jnp.pad(state_pool[l], ((0, 0), (1, 0), (0, 0)))
        xs, st_s = _trunk_layer(xs, mods_s, lw, True, bs, t, state_ret[l], pool_prev, caches_t, l,
                                PAST_LEN + jnp.arange(t), final, final_g)
        outs_p.append(st_p)
        outs_s.append(st_s)

    def stack(outs, pick):
        return jnp.stack([pick(o) for o in outs])

    res = [xp.reshape(bp, s, D), xs.reshape(bs, t, D),
           stack(outs_p, lambda o: o[0]), stack(outs_s, lambda o: o[0]),
           stack(outs_p, lambda o: o[1]), stack(outs_s, lambda o: o[1])]
    for gi in range(len(ATT_PATTERNS)):
        res.append(stack(outs_p, lambda o: o[2][gi]))
        res.append(stack(outs_s, lambda o: o[2][gi]))
    return tuple(res)
```

```python
import functools
import math

import jax
import jax.numpy as jnp
from jax import lax
from jax.experimental import pallas as pl
from jax.experimental.pallas import tpu as pltpu

F32 = jnp.float32
BF16 = jnp.bfloat16

D = 1024
RMS_EPS = 1e-6
GN_EPS = 1e-5
PAST_LEN = 2048

RET_HEADS = 4
RET_DK = 64
RET_DV = 128
RET_CHUNK = 128
ROPE_BASE = 10000.0
RET_LOG_GAMMA = tuple(math.log(1.0 - 2.0 ** (-5.0 - h)) for h in range(RET_HEADS))

POOL_WINDOWS = (2, 4, 8, 16)
POOL_GW = 128
POOL_HALO = 16

ATT_PATTERNS = ((128, 1), (512, 4), (2048, 16))
ATT_HEADS = 4
ATT_DH = 64
ATT_BLOCK = 128
ATT_PAIR = 2
ATT_BACK = 128
NEG = -1e30

PEER_HEADS = 8
PEER_NKEYS = 128
PEER_EXPERTS = PEER_NKEYS * PEER_NKEYS
PEER_TOPK = 16
PEER_TILE = 8 * PEER_NKEYS
LANE_PAD = 128

IN_WIDTH = 7424
CB = 256
COL_GATE = 0
COL_QR, COL_KR, COL_VR, COL_GR, COL_UP = 12, 13, 14, 16, 18
COL_QA, COL_KA, COL_VA = 20, 23, 26
N_CB = IN_WIDTH // CB

VMEM_LIMIT = 60 * 1024 * 1024


def _cparams(sem):
    return pltpu.CompilerParams(dimension_semantics=sem, vmem_limit_bytes=VMEM_LIMIT)


def _mod_spec(per_row, tm, tiles_per_batch, grid_pos):
    if per_row:
        return pl.BlockSpec((1, tm, D), lambda *g: (g[grid_pos], 0, 0))
    return pl.BlockSpec((1, 1, D), lambda *g: (g[grid_pos] // tiles_per_batch, 0, 0))


def _rms_mod(x, g, sh, sc):
    ms = jnp.mean(x * x, axis=-1, keepdims=True)
    h = x * lax.rsqrt(ms + RMS_EPS) * g
    return h * (1.0 + sc) + sh


def _ada_kernel(c_ref, w_ref, b_ref, o_ref):
    c = c_ref[...]
    s = c * jax.nn.sigmoid(c)
    o_ref[0] = jnp.dot(s.astype(BF16), w_ref[0].astype(BF16), preferred_element_type=F32) + b_ref[0]


def _adaln(c_all, ada_w, ada_b):
    depth, _, n = ada_w.shape
    m = c_all.shape[0]
    tn = 512
    return pl.pallas_call(
        _ada_kernel,
        grid=(depth, n // tn),
        in_specs=[pl.BlockSpec((m, D), lambda l, j: (0, 0)),
                  pl.BlockSpec((1, D, tn), lambda l, j: (l, 0, j)),
                  pl.BlockSpec((1, 1, tn), lambda l, j: (l, 0, j))],
        out_specs=pl.BlockSpec((1, m, tn), lambda l, j: (l, 0, j)),
        out_shape=jax.ShapeDtypeStruct((depth, m, n), F32),
        compiler_params=_cparams(("arbitrary", "arbitrary")),
        name="adaln",
    )(c_all, ada_w, ada_b.reshape(depth, 1, n))


def _inproj_kernel(x_ref, g_ref, sh_ref, sc_ref, w_ref, o_ref):
    h = _rms_mod(x_ref[...], g_ref[...], sh_ref[0], sc_ref[0])
    o_ref[...] = jnp.dot(h.astype(BF16), w_ref[...], preferred_element_type=F32)


def _inproj(x, g, sh, sc, w_bf, per_row, rows_per_batch):
    m = x.shape[0]
    tm = 512
    tn = IN_WIDTH // 2
    tpb = max(rows_per_batch // tm, 1)
    return pl.pallas_call(
        _inproj_kernel,
        grid=(2, m // tm),
        in_specs=[pl.BlockSpec((tm, D), lambda j, i: (i, 0)),
                  pl.BlockSpec((1, D), lambda j, i: (0, 0)),
                  _mod_spec(per_row, tm, tpb, 1),
                  _mod_spec(per_row, tm, tpb, 1),
                  pl.BlockSpec((D, tn), lambda j, i: (0, j))],
        out_specs=pl.BlockSpec((tm, tn), lambda j, i: (i, j)),
        out_shape=jax.ShapeDtypeStruct((m, IN_WIDTH), F32),
        compiler_params=_cparams(("arbitrary", "arbitrary")),
        name="inproj",
    )(x, g, sh, sc, w_bf)


def _ret_kernel(q_ref, k_ref, v_ref, g_ref, cos_ref, sin_ref, st0_ref, gng_ref, gnb_ref,
                y_ref, st_ref, st_scr, *, C):
    c = pl.program_id(1)

    @pl.when(c == 0)
    def _():
        st_scr[...] = st0_ref[0]

    cos = cos_ref[...]
    sin = sin_ref[...]
    lane = lax.broadcasted_iota(jnp.int32, (C, RET_HEADS * RET_DK), 1)
    first_half = (lane % RET_DK) < (RET_DK // 2)

    def rot(x):
        nl = RET_HEADS * RET_DK
        swapped = jnp.where(first_half, pltpu.roll(x, nl - RET_DK // 2, 1), pltpu.roll(x, RET_DK // 2, 1))
        return x * cos + swapped * sin

    q = rot(q_ref[0])
    k = rot(k_ref[0]) * (RET_DK ** -0.5)
    v = v_ref[0]
    g = g_ref[0]
    ii = lax.broadcasted_iota(jnp.int32, (C, C), 0)
    jj = lax.broadcasted_iota(jnp.int32, (C, C), 1)
    diff = (ii - jj).astype(F32)
    causal = ii >= jj
    row = lax.broadcasted_iota(jnp.int32, (C, 1), 0).astype(F32)
    for h in range(RET_HEADS):
        lg = RET_LOG_GAMMA[h]
        qh = q[:, h * RET_DK:(h + 1) * RET_DK].astype(BF16)
        khf = k[:, h * RET_DK:(h + 1) * RET_DK]
        vh = v[:, h * RET_DV:(h + 1) * RET_DV].astype(BF16)
        st = st_scr[h]
        decay = jnp.where(causal, jnp.exp(lg * jnp.maximum(diff, 0.0)), 0.0)
        s = lax.dot_general(qh, khf.astype(BF16), (((1,), (1,)), ((), ())), preferred_element_type=F32) * decay
        o = jnp.dot(s.astype(BF16), vh, preferred_element_type=F32)
        o = o + jnp.dot(qh, st.astype(BF16), preferred_element_type=F32) * jnp.exp(lg * (row + 1.0))
        kdec = (khf * jnp.exp(lg * (C - 1.0 - row))).astype(BF16)
        st_scr[h] = math.exp(lg * C) * st + lax.dot_general(
            kdec, vh, (((0,), (0,)), ((), ())), preferred_element_type=F32)
        mu = jnp.mean(o, axis=-1, keepdims=True)
        var = jnp.mean(jnp.square(o - mu), axis=-1, keepdims=True)
        on = (o - mu) * lax.rsqrt(var + GN_EPS)
        on = on * gng_ref[:, h * RET_DV:(h + 1) * RET_DV] + gnb_ref[:, h * RET_DV:(h + 1) * RET_DV]
        gh = g[:, h * RET_DV:(h + 1) * RET_DV]
        y_ref[0, :, h * RET_DV:(h + 1) * RET_DV] = gh * jax.nn.sigmoid(gh) * on

    @pl.when(c == pl.num_programs(1) - 1)
    def _():
        st_ref[0] = st_scr[...]


def _retention(z3, st0, cos2, sin2, gn_g, gn_b, C):
    b, t, _ = z3.shape
    nc = t // C
    vq = RET_HEADS * RET_DK
    vv = RET_HEADS * RET_DV
    return pl.pallas_call(
        functools.partial(_ret_kernel, C=C),
        grid=(b, nc),
        in_specs=[pl.BlockSpec((1, C, vq), lambda i, c: (i, c, COL_QR)),
                  pl.BlockSpec((1, C, vq), lambda i, c: (i, c, COL_KR)),
                  pl.BlockSpec((1, C, vv), lambda i, c: (i, c, COL_VR // 2)),
                  pl.BlockSpec((1, C, vv), lambda i, c: (i, c, COL_GR // 2)),
                  pl.BlockSpec((C, vq), lambda i, c: (c, 0)),
                  pl.BlockSpec((C, vq), lambda i, c: (c, 0)),
                  pl.BlockSpec((1, RET_HEADS, RET_DK, RET_DV), lambda i, c: (i, 0, 0, 0)),
                  pl.BlockSpec((1, vv), lambda i, c: (0, 0)),
                  pl.BlockSpec((1, vv), lambda i, c: (0, 0))],
        out_specs=[pl.BlockSpec((1, C, vv), lambda i, c: (i, c, 0)),
                   pl.BlockSpec((1, RET_HEADS, RET_DK, RET_DV), lambda i, c: (i, 0, 0, 0))],
        out_shape=[jax.ShapeDtypeStruct((b, t, vv), F32),
                   jax.ShapeDtypeStruct((b, RET_HEADS, RET_DK, RET_DV), F32)],
        scratch_shapes=[pltpu.VMEM((RET_HEADS, RET_DK, RET_DV), F32)],
        compiler_params=_cparams(("arbitrary", "arbitrary")),
        name="retention",
    )(z3, z3, z3, z3, cos2, sin2, st0, gn_g, gn_b)


def _rope_tables(pos):
    half = RET_DK // 2
    inv = 1.0 / (ROPE_BASE ** jnp.linspace(0.0, 1.0, half, dtype=F32))
    ang = pos.astype(F32)[:, None] * inv[None, :]
    cos, sin = jnp.cos(ang), jnp.sin(ang)
    cos2 = jnp.tile(jnp.concatenate([cos, cos], axis=-1), (1, RET_HEADS))
    sin2 = jnp.tile(jnp.concatenate([-sin, sin], axis=-1), (1, RET_HEADS))
    return cos2, sin2


def _pool_kernel(prev_ref, cur_ref, p_ref, *, tm, pos0, zero_first):
    i = pl.program_id(1)
    cur = cur_ref[0]
    prev = prev_ref[0]
    if zero_first:
        prev = jnp.where(i == 0, 0.0, prev)
    ext = jnp.concatenate([prev, cur], axis=0)
    row = lax.broadcasted_iota(jnp.int32, (tm, 1), 0) + i * tm + pos0
    acc = ext
    sums = []
    for s in (1, 2, 4, 8):
        acc = acc + pltpu.roll(acc, s, 0)
        sums.append(acc)
    for gi, w in enumerate(POOL_WINDOWS):
        sl = slice(gi * POOL_GW, (gi + 1) * POOL_GW)
        win = sums[gi][POOL_HALO:, sl]
        cnt = jnp.minimum(row + 1, w).astype(F32)
        p_ref[0, :, sl] = win / cnt - cur[:, sl]


def _pool(z3, prev, tm, pos0, zero_first):
    b, t, _ = z3.shape
    nt = t // tm
    width = 4 * POOL_GW
    if prev is None:
        prev_arr = z3
        hb = tm // POOL_HALO
        prev_spec = pl.BlockSpec((1, POOL_HALO, width), lambda bi, i: (bi, jnp.maximum(i * hb - 1, 0), COL_UP // 2))
    else:
        prev_arr = prev
        prev_spec = pl.BlockSpec((1, POOL_HALO, width), lambda bi, i: (bi, 0, 0))
    return pl.pallas_call(
        functools.partial(_pool_kernel, tm=tm, pos0=pos0, zero_first=zero_first),
        grid=(b, nt),
        in_specs=[prev_spec, pl.BlockSpec((1, tm, width), lambda bi, i: (bi, i, COL_UP // 2))],
        out_specs=pl.BlockSpec((1, tm, width), lambda bi, i: (bi, i, 0)),
        out_shape=jax.ShapeDtypeStruct((b, t, width), F32),
        compiler_params=_cparams(("arbitrary", "arbitrary")),
        name="pool",
    )(prev_arr, z3)


def _datt_kernel(q_ref, kc_ref, kp_ref, vc_ref, vp_ref, o_ref, l_ref, *, dil):
    n = pl.program_id(1)
    qb = ATT_BLOCK
    ii = lax.broadcasted_iota(jnp.int32, (qb, qb), 0)
    jj = lax.broadcasted_iota(jnp.int32, (qb, qb), 1)
    mask_c = jj <= ii
    mask_p = (jj >= ii) & (n > 0)
    nt = (((1,), (1,)), ((), ()))

    def residue(r, carry):
        rows = pl.ds(r, qb, stride=dil) if dil > 1 else pl.ds(0, qb)
        q = q_ref[0, rows, :] * (ATT_DH ** -0.5)
        kc, kp, vc, vp = kc_ref[0, rows, :], kp_ref[0, rows, :], vc_ref[0, rows, :], vp_ref[0, rows, :]
        outs, lses = [], []
        for h in range(ATT_PAIR):
            sl = slice(h * ATT_DH, (h + 1) * ATT_DH)
            qh = q[:, sl].astype(BF16)
            s_c = jnp.where(mask_c, lax.dot_general(qh, kc[:, sl].astype(BF16), nt, preferred_element_type=F32), NEG)
            s_p = jnp.where(mask_p, lax.dot_general(qh, kp[:, sl].astype(BF16), nt, preferred_element_type=F32), NEG)
            m = jnp.maximum(jnp.max(s_c, axis=-1, keepdims=True), jnp.max(s_p, axis=-1, keepdims=True))
            p_c = jnp.exp(s_c - m)
            p_p = jnp.exp(s_p - m)
            den = jnp.sum(p_c, axis=-1, keepdims=True) + jnp.sum(p_p, axis=-1, keepdims=True)
            num = (jnp.dot(p_c.astype(BF16), vc[:, sl].astype(BF16), preferred_element_type=F32)
                   + jnp.dot(p_p.astype(BF16), vp[:, sl].astype(BF16), preferred_element_type=F32))
            outs.append(num / den)
            lses.append(jnp.broadcast_to(m + jnp.log(den), (qb, ATT_DH)))
        o_ref[0, rows, :] = jnp.concatenate(outs, axis=-1)
        l_ref[0, rows, :] = jnp.concatenate(lses, axis=-1)
        return carry

    if dil == 1:
        residue(0, 0)
    else:
        lax.fori_loop(0, dil, residue, 0)


def _datt_prompt(z3, gi, dil):
    b, s, _ = z3.shape
    tb = ATT_BLOCK * dil
    w = ATT_HEADS * ATT_DH
    wp = ATT_PAIR * ATT_DH
    npair = ATT_HEADS // ATT_PAIR

    def spec(col, prev):
        if prev:
            return pl.BlockSpec((1, tb, wp), lambda bi, n, hp: (bi, jnp.maximum(n - 1, 0), (col + gi) * npair + hp))
        return pl.BlockSpec((1, tb, wp), lambda bi, n, hp: (bi, n, (col + gi) * npair + hp))

    o, lse = pl.pallas_call(
        functools.partial(_datt_kernel, dil=dil),
        grid=(b, s // tb, npair),
        in_specs=[spec(COL_QA, False), spec(COL_KA, False), spec(COL_KA, True),
                  spec(COL_VA, False), spec(COL_VA, True)],
        out_specs=[pl.BlockSpec((1, tb, wp), lambda bi, n, hp: (bi, n, hp))] * 2,
        out_shape=[jax.ShapeDtypeStruct((b, s, w), F32)] * 2,
        compiler_params=_cparams(("arbitrary", "arbitrary", "arbitrary")),
        name=f"datt_prompt_d{dil}",
    )(z3, z3, z3, z3, z3)
    return o.reshape(b * s, w), lse.reshape(b * s, w)


def _satt_kernel(*refs, T):
    ng = len(ATT_PATTERNS)
    qkv = refs[:3 * ng]
    bufs = refs[3 * ng:4 * ng]
    outs = refs[4 * ng:]
    nt = (((1,), (1,)), ((), ()))
    for gi, (win, dil) in enumerate(ATT_PATTERNS):
        q = qkv[gi][0] * (ATT_DH ** -0.5)
        kn = qkv[ng + gi][0]
        vn = qkv[2 * ng + gi][0]
        buf = bufs[gi]
        Wb = buf.shape[-1]
        qi_b = lax.broadcasted_iota(jnp.int32, (T, Wb), 0)
        dist_b = Wb + qi_b - lax.broadcasted_iota(jnp.int32, (T, Wb), 1)
        ok_b = (dist_b % dil == 0) & (dist_b <= dil * ATT_BACK)
        dist_n = lax.broadcasted_iota(jnp.int32, (T, T), 0) - lax.broadcasted_iota(jnp.int32, (T, T), 1)
        ok_n = (dist_n >= 0) & (dist_n % dil == 0)
        o_parts, l_parts = [], []
        for h in range(ATT_HEADS):
            sl = slice(h * ATT_DH, (h + 1) * ATT_DH)
            qh = q[:, sl].astype(BF16)
            kt = buf[0, 0, h].astype(BF16)
            vt = buf[0, 1, h].astype(BF16)
            s_b = jnp.where(ok_b, jnp.dot(qh, kt, preferred_element_type=F32), NEG)
            s_n = jnp.where(ok_n, lax.dot_general(qh, kn[:, sl].astype(BF16), nt, preferred_element_type=F32), NEG)
            m = jnp.maximum(jnp.max(s_b, axis=-1, keepdims=True), jnp.max(s_n, axis=-1, keepdims=True))
            p_b = jnp.exp(s_b - m)
            p_n = jnp.exp(s_n - m)
            den = jnp.sum(p_b, axis=-1, keepdims=True) + jnp.sum(p_n, axis=-1, keepdims=True)
            num = (lax.dot_general(p_b.astype(BF16), vt, nt, preferred_element_type=F32)
                   + jnp.dot(p_n.astype(BF16), vn[:, sl].astype(BF16), preferred_element_type=F32))
            o_parts.append(num / den)
            l_parts.append(jnp.broadcast_to(m + jnp.log(den), (T, ATT_DH)))
        outs[2 * gi][0] = jnp.concatenate(o_parts, axis=-1)
        outs[2 * gi + 1][0] = jnp.concatenate(l_parts, axis=-1)


def _datt_sample(z3, bufs_t, layer):
    b, t, _ = z3.shape
    w = ATT_HEADS * ATT_DH
    ng = len(ATT_PATTERNS)
    col = lambda c: pl.BlockSpec((1, t, w), lambda i: (i, 0, c))
    in_specs = ([col(COL_QA + gi) for gi in range(ng)] + [col(COL_KA + gi) for gi in range(ng)]
                + [col(COL_VA + gi) for gi in range(ng)]
                + [pl.BlockSpec((None, 1) + bt.shape[2:], lambda i: (layer, i, 0, 0, 0, 0)) for bt in bufs_t])
    res = pl.pallas_call(
        functools.partial(_satt_kernel, T=t),
        grid=(b,),
        in_specs=in_specs,
        out_specs=[pl.BlockSpec((1, t, w), lambda i: (i, 0, 0))] * (2 * ng),
        out_shape=[jax.ShapeDtypeStruct((b, t, w), F32)] * (2 * ng),
        compiler_params=_cparams(("arbitrary",)),
        name="datt_sample",
    )(*([z3] * (3 * ng)), *bufs_t)
    return [(res[2 * gi].reshape(b * t, w), res[2 * gi + 1].reshape(b * t, w)) for gi in range(ng)]


def _mix_kernel(x_ref, ga_ref, yr_ref, p_ref, o1_ref, l1_ref, o2_ref, l2_ref, o3_ref, l3_ref,
                gr_ref, gp_ref, gatt_ref, wr_ref, wp_ref, ps_ref, wa_ref, wo_ref, out_ref):
    y_r = jnp.dot(yr_ref[...].astype(BF16), wr_ref[...], preferred_element_type=F32)
    p = p_ref[...]
    parts = []
    for gi in range(len(POOL_WINDOWS)):
        parts.append(jnp.dot(p[:, gi * POOL_GW:(gi + 1) * POOL_GW].astype(BF16), wp_ref[gi],
                             preferred_element_type=F32))
    y_p = jnp.concatenate(parts, axis=-1) * ps_ref[...]
    l1, l2, l3 = l1_ref[...], l2_ref[...], l3_ref[...]
    lm = jnp.maximum(jnp.maximum(l1, l2), l3)
    w1, w2, w3 = jnp.exp(l1 - lm), jnp.exp(l2 - lm), jnp.exp(l3 - lm)
    att = (w1 * o1_ref[...] + w2 * o2_ref[...] + w3 * o3_ref[...]) / (w1 + w2 + w3)
    y_a = jnp.dot(att.astype(BF16), wa_ref[...], preferred_element_type=F32)
    mix = (jax.nn.sigmoid(gr_ref[...]) * y_r + jax.nn.sigmoid(gp_ref[...]) * y_p
           + jax.nn.sigmoid(gatt_ref[...]) * y_a)
    out = jnp.dot(mix.astype(BF16), wo_ref[...], preferred_element_type=F32)
    out_ref[...] = x_ref[...] + ga_ref[0] * out


def _mix(x, ga, z, yr, p, att_parts, w_ret_out, pool_w, pool_scale, w_att_out, w_out, per_row, rows_per_batch):
    m = x.shape[0]
    tm = 256
    tpb = max(rows_per_batch // tm, 1)
    row = lambda width: pl.BlockSpec((tm, width), lambda i: (i, 0))
    full = lambda shape: pl.BlockSpec(shape, lambda i: (0,) * len(shape))
    gate = lambda k: pl.BlockSpec((tm, D), lambda i: (i, k))
    (o1, l1), (o2, l2), (o3, l3) = att_parts
    return pl.pallas_call(
        _mix_kernel,
        grid=(m // tm,),
        in_specs=[row(D), _mod_spec(per_row, tm, tpb, 0), row(512), row(512),
                  row(256), row(256), row(256), row(256), row(256), row(256),
                  gate(0), gate(1), gate(2),
                  full((512, D)), full((4, POOL_GW, 256)), full((1, D)), full((256, D)), full((D, D))],
        out_specs=row(D),
        out_shape=jax.ShapeDtypeStruct((m, D), F32),
        compiler_params=_cparams(("arbitrary",)),
        name="mix",
    )(x, ga, yr, p, o1, l1, o2, l2, o3, l3, z, z, z, w_ret_out, pool_w, pool_scale, w_att_out, w_out)


def _split_bf16(x):
    hi = x.astype(BF16)
    return hi, (x - hi.astype(F32)).astype(BF16)


def _dot3(a_hi, a_lo, b_hi, b_lo):
    d = functools.partial(jnp.dot, preferred_element_type=F32)
    return d(a_hi, b_hi) + (d(a_hi, b_lo) + d(a_lo, b_hi))


def _sorting_network(n):
    pairs, p = [], 1
    while p < n:
        k = p
        while k >= 1:
            for j in range(k % p, n - k, 2 * k):
                for i in range(min(k, n - j - k)):
                    if (i + j) // (2 * p) == (i + j + k) // (2 * p):
                        pairs.append((i + j, i + j + k))
            k //= 2
        p *= 2
    return tuple(pairs)


_SORT16 = _sorting_network(PEER_TOPK)


def _extract_top(s, v_scr, tm):
    nt8 = PEER_NKEYS // 8
    for c in range(tm // 128):
        cs = slice(c * 128, (c + 1) * 128)
        a = [s[8 * j:8 * (j + 1), cs] for j in range(nt8)]
        for i, j in _SORT16:
            a[i], a[j] = jnp.maximum(a[i], a[j]), jnp.minimum(a[i], a[j])
        for k in range(PEER_TOPK):
            mx = jnp.max(a[0], axis=0, keepdims=True)
            v_scr[k:k + 1, cs] = mx
            pop = a[0] == mx
            last = PEER_TOPK - 1 - k
            for j in range(last):
                a[j] = jnp.where(pop, a[j + 1], a[j])
            a[last] = jnp.where(pop, NEG, a[last])


def _route_kernel(x_ref, g_ref, sh_ref, sc_ref, wqh_ref, wql_ref, k1h_ref, k1l_ref, k2h_ref, k2l_ref,
                  ht_ref, e1_ref, th_ref, e2_ref, v1_scr, v2_scr, *, tm):
    h2 = _rms_mod(x_ref[...], g_ref[...], sh_ref[0], sc_ref[0])
    ht = h2.T
    ht_hi, ht_lo = _split_bf16(ht)
    ht_ref[0] = ht_hi
    qt = _dot3(wqh_ref[...], wql_ref[...], ht_hi, ht_lo)
    half = PEER_NKEYS // 2
    sub = lax.broadcasted_iota(jnp.int32, (8, tm), 0)
    none = 2.0
    for h in range(PEER_HEADS):
        base = h * PEER_NKEYS
        q1h, q1l = _split_bf16(qt[base:base + half])
        q2h, q2l = _split_bf16(qt[base + half:base + PEER_NKEYS])
        s1 = _dot3(k1h_ref[h], k1l_ref[h], q1h, q1l)
        s2 = _dot3(k2h_ref[h], k2l_ref[h], q2h, q2l)
        e1 = jnp.exp(s1 - jnp.max(s1, axis=0, keepdims=True))
        e2 = jnp.exp(s2 - jnp.max(s2, axis=0, keepdims=True))
        _extract_top(e1, v1_scr, tm)
        _extract_top(e2, v2_scr, tm)
        ev1 = v1_scr[...]
        ev2 = v2_scr[...]
        blocks = [ev1[0:1] * ev2, ev1[1:2] * ev2[0:8]]
        for a in range(2, 8):
            blocks.append(jnp.where(sub < PEER_TOPK // (a + 1), ev1[a:a + 1] * ev2[0:8], -1.0))
        blocks.append(ev1[8:16] * ev2[0:1])
        cand = jnp.concatenate(blocks, axis=0)
        rem = cand
        thr = None
        for _ in range(PEER_TOPK):
            thr = jnp.max(rem, axis=0, keepdims=True)
            rem = jnp.where(rem == thr, -1.0, rem)
        sel = cand >= thr
        z = jnp.sum(jnp.where(sel, cand, 0.0), axis=0, keepdims=True)
        rz = 1.0 / z
        low = jnp.where(sel, jnp.concatenate([ev2] + [ev2[0:8]] * 7 + [jnp.broadcast_to(ev2[0:1], (8, tm))], axis=0),
                        none)
        th_a = [jnp.min(low[0:16], axis=0, keepdims=True)]
        for a in range(1, 8):
            th_a.append(jnp.min(low[8 + 8 * a:16 + 8 * a], axis=0, keepdims=True))
        for a in range(8, 16):
            th_a.append(low[72 + a - 8:73 + a - 8])
        thmap = jnp.full((PEER_NKEYS, tm), none, F32)
        for a in range(PEER_TOPK):
            thmap = jnp.where(e1 == ev1[a:a + 1], th_a[a], thmap)
        rows = slice(base, base + PEER_NKEYS)
        for ref, val in ((e1_ref, e1 * rz), (th_ref, thmap), (e2_ref, e2)):
            ref[0, rows, 0:tm] = val
            ref[0, rows, tm:] = jnp.zeros((PEER_NKEYS, LANE_PAD), F32)


def _route(x, g, sh, sc, lw, per_row, rows_per_batch, tm):
    m = x.shape[0]
    nt = m // tm
    tpb = max(rows_per_batch // tm, 1)
    hk = PEER_HEADS * PEER_NKEYS
    full = lambda a: pl.BlockSpec(a.shape, lambda i: (0,) * a.ndim)
    tile = lambda r: pl.BlockSpec((1, r, tm), lambda i: (i, 0, 0))
    ws = [lw["wq_hi"], lw["wq_lo"], lw["k1_hi"], lw["k1_lo"], lw["k2_hi"], lw["k2_lo"]]
    return pl.pallas_call(
        functools.partial(_route_kernel, tm=tm),
        grid=(nt,),
        in_specs=[pl.BlockSpec((tm, D), lambda i: (i, 0)), full(g),
                  _mod_spec(per_row, tm, tpb, 0), _mod_spec(per_row, tm, tpb, 0)] + [full(a) for a in ws],
        out_specs=[tile(D)] + [pl.BlockSpec((1, hk, tm + LANE_PAD), lambda i: (i, 0, 0))] * 3,
        out_shape=[jax.ShapeDtypeStruct((nt, D, tm), BF16)]
        + [jax.ShapeDtypeStruct((nt, hk, tm + LANE_PAD), F32)] * 3,
        scratch_shapes=[pltpu.VMEM((PEER_TOPK, tm), F32), pltpu.VMEM((PEER_TOPK, tm), F32)],
        compiler_params=_cparams(("arbitrary",)),
        name="peer_route",
    )(x, g, sh, sc, *ws)


def _gelu_tanh(x):
    hx = 0.5 * x
    t = (x * x) * (0.7978845608028654 * 0.044715) + 0.7978845608028654
    return hx + hx * jnp.tanh(x * t)


def _experts_kernel(ht_ref, e1_ref, th_ref, e2_ref, u_ref, vt_ref, x_ref, ga_ref, fg_ref,
                    out_ref, acc_ref, a_ref, h_ref, *, tm, final):
    e = pl.program_id(1)
    ne = pl.num_programs(1) - 1
    te = PEER_TILE
    lc = 128
    n_i1 = te // PEER_NKEYS

    def first_matmul():
        a_ref[:, 0:tm] = jnp.dot(u_ref[...], ht_ref[0], preferred_element_type=F32)

    def second_matmul():
        acc_ref[:, 0:tm] += jnp.dot(vt_ref[0], h_ref[(e + 1) % 2], preferred_element_type=F32)

    def gate():
        slot = e % 2
        for r in range(n_i1):
            for c in range(tm // lc):
                cs = slice(c * lc, (c + 1) * lc)
                w = None
                for h in range(PEER_HEADS):
                    start = pl.multiple_of(h * PEER_NKEYS + e * n_i1, n_i1)
                    e1row = e1_ref[0, pl.ds(start, n_i1), cs][r:r + 1]
                    throw = th_ref[0, pl.ds(start, n_i1), cs][r:r + 1]
                    e2 = e2_ref[0, h * PEER_NKEYS:(h + 1) * PEER_NKEYS, cs]
                    term = jnp.where(e2 >= throw, e1row * e2, 0.0)
                    w = term if w is None else w + term
                rs = slice(r * PEER_NKEYS, (r + 1) * PEER_NKEYS)
                h_ref[slot, rs, cs] = (w * _gelu_tanh(a_ref[rs, cs])).astype(BF16)

    @pl.when(e == 0)
    def _():
        acc_ref[...] = jnp.zeros_like(acc_ref)
        first_matmul()
        gate()

    @pl.when((e > 0) & (e < ne))
    def _():
        first_matmul()
        second_matmul()
        gate()

    @pl.when(e == ne)
    def _():
        second_matmul()
        y = x_ref[...] + ga_ref[0] * acc_ref[:, 0:tm].T
        if final:
            ms = jnp.mean(y * y, axis=-1, keepdims=True)
            y = y * lax.rsqrt(ms + RMS_EPS) * fg_ref[...]
        out_ref[...] = y


def _experts(ht, e1, th, e2, u_bf, vt_bf, x, ga, fg, per_row, rows_per_batch, tm, final):
    m = x.shape[0]
    nt = m // tm
    te = PEER_TILE
    ne = PEER_EXPERTS // te
    tpb = max(rows_per_batch // tm, 1)
    hk = PEER_HEADS * PEER_NKEYS
    tmp = tm + LANE_PAD
    sel = pl.BlockSpec((1, hk, tmp), lambda i, e: (i, 0, 0))
    return pl.pallas_call(
        functools.partial(_experts_kernel, tm=tm, final=final),
        grid=(nt, ne + 1),
        in_specs=[pl.BlockSpec((1, D, tm), lambda i, e: (i, 0, 0)), sel, sel, sel,
                  pl.BlockSpec((te, D), lambda i, e: (jnp.minimum(e, ne - 1), 0)),
                  pl.BlockSpec((1, D, te), lambda i, e: (jnp.maximum(e - 1, 0), 0, 0)),
                  pl.BlockSpec((tm, D), lambda i, e: (i, 0)),
                  _mod_spec(per_row, tm, tpb, 0),
                  pl.BlockSpec((1, D), lambda i, e: (0, 0))],
        out_specs=pl.BlockSpec((tm, D), lambda i, e: (i, 0)),
        out_shape=jax.ShapeDtypeStruct((m, D), F32),
        scratch_shapes=[pltpu.VMEM((D, tmp), F32), pltpu.VMEM((te, tmp), F32), pltpu.VMEM((2, te, tm), BF16)],
        compiler_params=_cparams(("arbitrary", "arbitrary")),
        name="peer_experts",
    )(ht, e1, th, e2, u_bf, vt_bf, x, ga, fg)


def _group_rows(per_row, vec, rows_per_batch, tm):
    if per_row:
        rows = jnp.repeat(vec, rows_per_batch, axis=0)
        return rows.reshape(rows.shape[0] // tm, tm, D)
    return vec[:, None, :]


def _trunk_layer(x, mods, lw, per_row, b, t, ret_st0, pool_prev, bufs, layer, pos, final, final_g):
    sh1, sc1, ga1, sh2, sc2, ga2 = mods
    m = b * t
    gm = lambda v, tm: _group_rows(per_row, v, t, tm)
    z = _inproj(x, lw["g_mix"], gm(sh1, 512), gm(sc1, 512), lw["w_in"], per_row, t)
    z3 = z.reshape(b, t, IN_WIDTH)

    cos2, sin2 = _rope_tables(pos)
    chunk = RET_CHUNK if t >= RET_CHUNK else t
    yr, new_ret = _retention(z3, ret_st0, cos2, sin2, lw["gn_g"], lw["gn_b"], chunk)

    if per_row:
        p = _pool(z3, pool_prev, t, PAST_LEN, False)
        att_parts = _datt_sample(z3, bufs, layer)
    else:
        p = _pool(z3, None, 512, 0, True)
        att_parts = [_datt_prompt(z3, gi, dil) for gi, (win, dil) in enumerate(ATT_PATTERNS)]

    x1 = _mix(x, gm(ga1, 256), z, yr.reshape(m, 512), p.reshape(m, 512), att_parts,
              lw["w_ret_out"], lw["pool_w"], lw["pool_scale"], lw["w_att_out"], lw["w_out"], per_row, t)

    tm = 512
    ht, e1, th, e2 = _route(x1, lw["g_ffn"], gm(sh2, tm), gm(sc2, tm), lw, per_row, t, tm)
    x2 = _experts(ht, e1, th, e2, lw["u"], lw["v_t"], x1, gm(ga2, tm), final_g, per_row, t, tm, final)

    new_kv = []
    for gi, (win, dil) in enumerate(ATT_PATTERNS):
        keep = t if per_row else min(win, t)
        kk = z3[:, t - keep:, (COL_KA + gi) * CB:(COL_KA + gi + 1) * CB].reshape(b, keep, ATT_HEADS, ATT_DH)
        vv = z3[:, t - keep:, (COL_VA + gi) * CB:(COL_VA + gi + 1) * CB].reshape(b, keep, ATT_HEADS, ATT_DH)
        new_kv.append(jnp.stack([kk, vv], axis=2))
    up = z3[:, :, COL_UP * CB:COL_UP * CB + 512]
    if per_row:
        new_pool = jnp.concatenate([pool_prev[:, 1:], up], axis=1)[:, -(POOL_HALO - 1):]
    else:
        new_pool = up[:, t - (POOL_HALO - 1):]
    return x2, (new_ret, new_pool, new_kv)


def kernel(x_prompt, x_sample, state_ret, state_pool, cache_kv_w128, cache_kv_w512, cache_kv_w2048,
           c_prompt, c_sample, ada_w, ada_b, norm_mix_g, norm_ffn_g, w_in, ret_gn_g, ret_gn_b,
           w_ret_out, pool_w, pool_scale, w_att_out, w_out, peer_wq, peer_keys, peer_u, peer_v,
           final_norm_g):
    depth = ada_w.shape[0]
    bp, s, _ = x_prompt.shape
    bs, t, _ = x_sample.shape
    caches_t = tuple(jnp.transpose(c, (0, 1, 3, 4, 5, 2)) for c in (cache_kv_w128, cache_kv_w512, cache_kv_w2048))

    mods_all = _adaln(jnp.concatenate([c_prompt, c_sample], axis=0), ada_w, ada_b)

    xp = x_prompt.reshape(bp * s, D)
    xs = x_sample.reshape(bs * t, D)
    final_g = final_norm_g.reshape(1, D)
    zero_state = jnp.zeros((bp, RET_HEADS, RET_DK, RET_DV), F32)
    outs_p, outs_s = [], []
    for l in range(depth):
        n_gate = 3 * D
        w_in_l = jnp.concatenate([w_in[l][:, IN_WIDTH - n_gate:], w_in[l][:, :IN_WIDTH - n_gate]], axis=1)
        wq_hi, wq_lo = _split_bf16(peer_wq[l].T)
        k1_hi, k1_lo = _split_bf16(peer_keys[l][:, 0])
        k2_hi, k2_lo = _split_bf16(peer_keys[l][:, 1])
        lw = {
            "g_mix": norm_mix_g[l].reshape(1, D), "g_ffn": norm_ffn_g[l].reshape(1, D),
            "w_in": w_in_l.astype(BF16),
            "gn_g": ret_gn_g[l].reshape(1, -1), "gn_b": ret_gn_b[l].reshape(1, -1),
            "w_ret_out": w_ret_out[l].astype(BF16), "pool_w": pool_w[l].astype(BF16),
            "pool_scale": pool_scale[l].reshape(1, D), "w_att_out": w_att_out[l].astype(BF16),
            "w_out": w_out[l].astype(BF16),
            "wq_hi": wq_hi, "wq_lo": wq_lo, "k1_hi": k1_hi, "k1_lo": k1_lo, "k2_hi": k2_hi, "k2_lo": k2_lo,
            "u": peer_u[l].astype(BF16),
            "v_t": peer_v[l].reshape(-1, PEER_TILE, D).transpose(0, 2, 1).astype(BF16),
        }
        ml = mods_all[l]
        mods_p = tuple(ml[:bp, k * D:(k + 1) * D] for k in range(6))
        mods_s = tuple(ml[bp:, k * D:(k + 1) * D] for k in range(6))
        final = l == depth - 1
        xp, st_p = _trunk_layer(xp, mods_p, lw, False, bp, s, zero_state, None, None, l,
                                jnp.arange(s), final, final_g)
        pool_prev = jnp.pad(state_pool[l], ((0, 0), (1, 0), (0, 0)))
        xs, st_s = _trunk_layer(xs, mods_s, lw, True, bs, t, state_ret[l], pool_prev, caches_t, l,
                                PAST_LEN + jnp.arange(t), final, final_g)
        outs_p.append(st_p)
        outs_s.append(st_s)

    def stack(outs, pick):
        return jnp.stack([pick(o) for o in outs])

    res = [xp.reshape(bp, s, D), xs.reshape(bs, t, D),
           stack(outs_p, lambda o: o[0]), stack(outs_s, lambda o: o[0]),
           stack(outs_p, lambda o: o[1]), stack(outs_s, lambda o: o[1])]
    for gi in range(len(ATT_PATTERNS)):
        res.append(stack(outs_p, lambda o: o[2][gi]))
        res.append(stack(outs_s, lambda o: o[2][gi]))
    return tuple(res)
```

```python
import functools
import math

import jax
import jax.numpy as jnp
from jax import lax
from jax.experimental import pallas as pl
from jax.experimental.pallas import tpu as pltpu

F32 = jnp.float32
BF16 = jnp.bfloat16

D = 1024
RMS_EPS = 1e-6
GN_EPS = 1e-5
PAST_LEN = 2048

RET_HEADS = 4
RET_DK = 64
RET_DV = 128
RET_CHUNK = 128
ROPE_BASE = 10000.0
RET_LOG_GAMMA = tuple(math.log(1.0 - 2.0 ** (-5.0 - h)) for h in range(RET_HEADS))

POOL_WINDOWS = (2, 4, 8, 16)
POOL_GW = 128
POOL_HALO = 16

ATT_PATTERNS = ((128, 1), (512, 4), (2048, 16))
ATT_HEADS = 4
ATT_DH = 64
ATT_BLOCK = 128
ATT_PAIR = 2
ATT_UNROLL = 4
ATT_BACK = 128
NEG = -1e30

PEER_HEADS = 8
PEER_NKEYS = 128
PEER_EXPERTS = PEER_NKEYS * PEER_NKEYS
PEER_TOPK = 16
PEER_TILE = 8 * PEER_NKEYS
LANE_PAD = 128

IN_WIDTH = 7424
CB = 256
COL_GATE = 0
COL_QR, COL_KR, COL_VR, COL_GR, COL_UP = 12, 13, 14, 16, 18
COL_QA, COL_KA, COL_VA = 20, 23, 26
N_CB = IN_WIDTH // CB

VMEM_LIMIT = 60 * 1024 * 1024


def _cparams(sem):
    return pltpu.CompilerParams(dimension_semantics=sem, vmem_limit_bytes=VMEM_LIMIT)


def _mod_spec(per_row, tm, tiles_per_batch, grid_pos):
    if per_row:
        return pl.BlockSpec((1, tm, D), lambda *g: (g[grid_pos], 0, 0))
    return pl.BlockSpec((1, 1, D), lambda *g: (g[grid_pos] // tiles_per_batch, 0, 0))


def _rms_mod(x, g, sh, sc):
    ms = jnp.mean(x * x, axis=-1, keepdims=True)
    h = x * lax.rsqrt(ms + RMS_EPS) * g
    return h * (1.0 + sc) + sh


def _ada_kernel(c_ref, w_ref, b_ref, o_ref):
    c = c_ref[...]
    s = c * jax.nn.sigmoid(c)
    o_ref[0] = jnp.dot(s.astype(BF16), w_ref[0].astype(BF16), preferred_element_type=F32) + b_ref[0]


def _adaln(c_all, ada_w, ada_b):
    depth, _, n = ada_w.shape
    m = c_all.shape[0]
    tn = 512
    return pl.pallas_call(
        _ada_kernel,
        grid=(depth, n // tn),
        in_specs=[pl.BlockSpec((m, D), lambda l, j: (0, 0)),
                  pl.BlockSpec((1, D, tn), lambda l, j: (l, 0, j)),
                  pl.BlockSpec((1, 1, tn), lambda l, j: (l, 0, j))],
        out_specs=pl.BlockSpec((1, m, tn), lambda l, j: (l, 0, j)),
        out_shape=jax.ShapeDtypeStruct((depth, m, n), F32),
        compiler_params=_cparams(("arbitrary", "arbitrary")),
        name="adaln",
    )(c_all, ada_w, ada_b.reshape(depth, 1, n))


def _inproj_kernel(x_ref, g_ref, sh_ref, sc_ref, w_ref, o_ref):
    h = _rms_mod(x_ref[...], g_ref[...], sh_ref[0], sc_ref[0])
    o_ref[...] = jnp.dot(h.astype(BF16), w_ref[...], preferred_element_type=F32)


def _inproj(x, g, sh, sc, w_bf, per_row, rows_per_batch):
    m = x.shape[0]
    tm = 512
    tn = IN_WIDTH // 2
    tpb = max(rows_per_batch // tm, 1)
    return pl.pallas_call(
        _inproj_kernel,
        grid=(2, m // tm),
        in_specs=[pl.BlockSpec((tm, D), lambda j, i: (i, 0)),
                  pl.BlockSpec((1, D), lambda j, i: (0, 0)),
                  _mod_spec(per_row, tm, tpb, 1),
                  _mod_spec(per_row, tm, tpb, 1),
                  pl.BlockSpec((D, tn), lambda j, i: (0, j))],
        out_specs=pl.BlockSpec((tm, tn), lambda j, i: (i, j)),
        out_shape=jax.ShapeDtypeStruct((m, IN_WIDTH), F32),
        compiler_params=_cparams(("arbitrary", "arbitrary")),
        name="inproj",
    )(x, g, sh, sc, w_bf)


def _ret_kernel(q_ref, k_ref, v_ref, g_ref, cos_ref, sin_ref, st0_ref, gng_ref, gnb_ref,
                y_ref, st_ref, st_scr, *, C):
    c = pl.program_id(1)

    @pl.when(c == 0)
    def _():
        st_scr[...] = st0_ref[0]

    cos = cos_ref[...]
    sin = sin_ref[...]
    lane = lax.broadcasted_iota(jnp.int32, (C, RET_HEADS * RET_DK), 1)
    first_half = (lane % RET_DK) < (RET_DK // 2)

    def rot(x):
        nl = RET_HEADS * RET_DK
        swapped = jnp.where(first_half, pltpu.roll(x, nl - RET_DK // 2, 1), pltpu.roll(x, RET_DK // 2, 1))
        return x * cos + swapped * sin

    q = rot(q_ref[0])
    k = rot(k_ref[0]) * (RET_DK ** -0.5)
    v = v_ref[0]
    g = g_ref[0]
    ii = lax.broadcasted_iota(jnp.int32, (C, C), 0)
    jj = lax.broadcasted_iota(jnp.int32, (C, C), 1)
    diff = (ii - jj).astype(F32)
    causal = ii >= jj
    row = lax.broadcasted_iota(jnp.int32, (C, 1), 0).astype(F32)
    for h in range(RET_HEADS):
        lg = RET_LOG_GAMMA[h]
        qh = q[:, h * RET_DK:(h + 1) * RET_DK].astype(BF16)
        khf = k[:, h * RET_DK:(h + 1) * RET_DK]
        vh = v[:, h * RET_DV:(h + 1) * RET_DV].astype(BF16)
        st = st_scr[h]
        decay = jnp.where(causal, jnp.exp(lg * jnp.maximum(diff, 0.0)), 0.0)
        s = lax.dot_general(qh, khf.astype(BF16), (((1,), (1,)), ((), ())), preferred_element_type=F32) * decay
        o = jnp.dot(s.astype(BF16), vh, preferred_element_type=F32)
        o = o + jnp.dot(qh, st.astype(BF16), preferred_element_type=F32) * jnp.exp(lg * (row + 1.0))
        kdec = (khf * jnp.exp(lg * (C - 1.0 - row))).astype(BF16)
        st_scr[h] = math.exp(lg * C) * st + lax.dot_general(
            kdec, vh, (((0,), (0,)), ((), ())), preferred_element_type=F32)
        mu = jnp.mean(o, axis=-1, keepdims=True)
        var = jnp.mean(jnp.square(o - mu), axis=-1, keepdims=True)
        on = (o - mu) * lax.rsqrt(var + GN_EPS)
        on = on * gng_ref[:, h * RET_DV:(h + 1) * RET_DV] + gnb_ref[:, h * RET_DV:(h + 1) * RET_DV]
        gh = g[:, h * RET_DV:(h + 1) * RET_DV]
        y_ref[0, :, h * RET_DV:(h + 1) * RET_DV] = gh * jax.nn.sigmoid(gh) * on

    @pl.when(c == pl.num_programs(1) - 1)
    def _():
        st_ref[0] = st_scr[...]


def _retention(z3, st0, cos2, sin2, gn_g, gn_b, C):
    b, t, _ = z3.shape
    nc = t // C
    vq = RET_HEADS * RET_DK
    vv = RET_HEADS * RET_DV
    return pl.pallas_call(
        functools.partial(_ret_kernel, C=C),
        grid=(b, nc),
        in_specs=[pl.BlockSpec((1, C, vq), lambda i, c: (i, c, COL_QR)),
                  pl.BlockSpec((1, C, vq), lambda i, c: (i, c, COL_KR)),
                  pl.BlockSpec((1, C, vv), lambda i, c: (i, c, COL_VR // 2)),
                  pl.BlockSpec((1, C, vv), lambda i, c: (i, c, COL_GR // 2)),
                  pl.BlockSpec((C, vq), lambda i, c: (c, 0)),
                  pl.BlockSpec((C, vq), lambda i, c: (c, 0)),
                  pl.BlockSpec((1, RET_HEADS, RET_DK, RET_DV), lambda i, c: (i, 0, 0, 0)),
                  pl.BlockSpec((1, vv), lambda i, c: (0, 0)),
                  pl.BlockSpec((1, vv), lambda i, c: (0, 0))],
        out_specs=[pl.BlockSpec((1, C, vv), lambda i, c: (i, c, 0)),
                   pl.BlockSpec((1, RET_HEADS, RET_DK, RET_DV), lambda i, c: (i, 0, 0, 0))],
        out_shape=[jax.ShapeDtypeStruct((b, t, vv), F32),
                   jax.ShapeDtypeStruct((b, RET_HEADS, RET_DK, RET_DV), F32)],
        scratch_shapes=[pltpu.VMEM((RET_HEADS, RET_DK, RET_DV), F32)],
        compiler_params=_cparams(("arbitrary", "arbitrary")),
        name="retention",
    )(z3, z3, z3, z3, cos2, sin2, st0, gn_g, gn_b)


def _rope_tables(pos):
    half = RET_DK // 2
    inv = 1.0 / (ROPE_BASE ** jnp.linspace(0.0, 1.0, half, dtype=F32))
    ang = pos.astype(F32)[:, None] * inv[None, :]
    cos, sin = jnp.cos(ang), jnp.sin(ang)
    cos2 = jnp.tile(jnp.concatenate([cos, cos], axis=-1), (1, RET_HEADS))
    sin2 = jnp.tile(jnp.concatenate([-sin, sin], axis=-1), (1, RET_HEADS))
    return cos2, sin2


def _pool_kernel(prev_ref, cur_ref, p_ref, *, tm, pos0, zero_first):
    i = pl.program_id(1)
    cur = cur_ref[0]
    prev = prev_ref[0]
    if zero_first:
        prev = jnp.where(i == 0, 0.0, prev)
    ext = jnp.concatenate([prev, cur], axis=0)
    row = lax.broadcasted_iota(jnp.int32, (tm, 1), 0) + i * tm + pos0
    acc = ext
    sums = []
    for s in (1, 2, 4, 8):
        acc = acc + pltpu.roll(acc, s, 0)
        sums.append(acc)
    for gi, w in enumerate(POOL_WINDOWS):
        sl = slice(gi * POOL_GW, (gi + 1) * POOL_GW)
        win = sums[gi][POOL_HALO:, sl]
        cnt = jnp.minimum(row + 1, w).astype(F32)
        p_ref[0, :, sl] = win / cnt - cur[:, sl]


def _pool(z3, prev, tm, pos0, zero_first):
    b, t, _ = z3.shape
    nt = t // tm
    width = 4 * POOL_GW
    if prev is None:
        prev_arr = z3
        hb = tm // POOL_HALO
        prev_spec = pl.BlockSpec((1, POOL_HALO, width), lambda bi, i: (bi, jnp.maximum(i * hb - 1, 0), COL_UP // 2))
    else:
        prev_arr = prev
        prev_spec = pl.BlockSpec((1, POOL_HALO, width), lambda bi, i: (bi, 0, 0))
    return pl.pallas_call(
        functools.partial(_pool_kernel, tm=tm, pos0=pos0, zero_first=zero_first),
        grid=(b, nt),
        in_specs=[prev_spec, pl.BlockSpec((1, tm, width), lambda bi, i: (bi, i, COL_UP // 2))],
        out_specs=pl.BlockSpec((1, tm, width), lambda bi, i: (bi, i, 0)),
        out_shape=jax.ShapeDtypeStruct((b, t, width), F32),
        compiler_params=_cparams(("arbitrary", "arbitrary")),
        name="pool",
    )(prev_arr, z3)


def _datt_kernel(q_ref, kc_ref, kp_ref, vc_ref, vp_ref, o_ref, l_ref, *, dil):
    n = pl.program_id(1)
    qb = ATT_BLOCK
    ii = lax.broadcasted_iota(jnp.int32, (qb, qb), 0)
    jj = lax.broadcasted_iota(jnp.int32, (qb, qb), 1)
    mask_c = jj <= ii
    mask_p = (jj >= ii) & (n > 0)
    nt = (((1,), (1,)), ((), ()))

    def residue(r, carry):
        rows = pl.ds(r, qb, stride=dil) if dil > 1 else pl.ds(0, qb)
        q = q_ref[0, rows, :] * (ATT_DH ** -0.5)
        kc, kp, vc, vp = kc_ref[0, rows, :], kp_ref[0, rows, :], vc_ref[0, rows, :], vp_ref[0, rows, :]
        outs, lses = [], []
        for h in range(q_ref.shape[-1] // ATT_DH):
            sl = slice(h * ATT_DH, (h + 1) * ATT_DH)
            qh = q[:, sl].astype(BF16)
            s_c = jnp.where(mask_c, lax.dot_general(qh, kc[:, sl].astype(BF16), nt, preferred_element_type=F32), NEG)
            s_p = jnp.where(mask_p, lax.dot_general(qh, kp[:, sl].astype(BF16), nt, preferred_element_type=F32), NEG)
            m = jnp.maximum(jnp.max(s_c, axis=-1, keepdims=True), jnp.max(s_p, axis=-1, keepdims=True))
            p_c = jnp.exp(s_c - m)
            p_p = jnp.exp(s_p - m)
            den = jnp.sum(p_c, axis=-1, keepdims=True) + jnp.sum(p_p, axis=-1, keepdims=True)
            num = (jnp.dot(p_c.astype(BF16), vc[:, sl].astype(BF16), preferred_element_type=F32)
                   + jnp.dot(p_p.astype(BF16), vp[:, sl].astype(BF16), preferred_element_type=F32))
            outs.append(num / den)
            lses.append(jnp.broadcast_to(m + jnp.log(den), (qb, ATT_DH)))
        o_ref[0, rows, :] = jnp.concatenate(outs, axis=-1)
        l_ref[0, rows, :] = jnp.concatenate(lses, axis=-1)
        return carry

    if dil == 1:
        residue(0, 0)
    else:
        lax.fori_loop(0, dil, residue, 0, unroll=min(dil, ATT_UNROLL))


def _datt_prompt(z3, gi, dil):
    b, s, _ = z3.shape
    tb = ATT_BLOCK * dil
    w = ATT_HEADS * ATT_DH
    heads = ATT_HEADS if dil == 1 else ATT_PAIR
    wp = heads * ATT_DH
    npair = ATT_HEADS // heads

    def spec(col, prev):
        if prev:
            return pl.BlockSpec((1, tb, wp), lambda bi, n, hp: (bi, jnp.maximum(n - 1, 0), (col + gi) * npair + hp))
        return pl.BlockSpec((1, tb, wp), lambda bi, n, hp: (bi, n, (col + gi) * npair + hp))

    o, lse = pl.pallas_call(
        functools.partial(_datt_kernel, dil=dil),
        grid=(b, s // tb, npair),
        in_specs=[spec(COL_QA, False), spec(COL_KA, False), spec(COL_KA, True),
                  spec(COL_VA, False), spec(COL_VA, True)],
        out_specs=[pl.BlockSpec((1, tb, wp), lambda bi, n, hp: (bi, n, hp))] * 2,
        out_shape=[jax.ShapeDtypeStruct((b, s, w), F32)] * 2,
        compiler_params=_cparams(("arbitrary", "arbitrary", "arbitrary")),
        name=f"datt_prompt_d{dil}",
    )(z3, z3, z3, z3, z3)
    return o.reshape(b * s, w), lse.reshape(b * s, w)


def _satt_kernel(*refs, T):
    ng = len(ATT_PATTERNS)
    qkv = refs[:3 * ng]
    bufs = refs[3 * ng:4 * ng]
    outs = refs[4 * ng:]
    nt = (((1,), (1,)), ((), ()))
    for gi, (win, dil) in enumerate(ATT_PATTERNS):
        q = qkv[gi][0] * (ATT_DH ** -0.5)
        kn = qkv[ng + gi][0]
        vn = qkv[2 * ng + gi][0]
        buf = bufs[gi]
        Wb = buf.shape[-1]
        qi_b = lax.broadcasted_iota(jnp.int32, (T, Wb), 0)
        dist_b = Wb + qi_b - lax.broadcasted_iota(jnp.int32, (T, Wb), 1)
        ok_b = (dist_b % dil == 0) & (dist_b <= dil * ATT_BACK)
        dist_n = lax.broadcasted_iota(jnp.int32, (T, T), 0) - lax.broadcasted_iota(jnp.int32, (T, T), 1)
        ok_n = (dist_n >= 0) & (dist_n % dil == 0)
        o_parts, l_parts = [], []
        for h in range(ATT_HEADS):
            sl = slice(h * ATT_DH, (h + 1) * ATT_DH)
            qh = q[:, sl].astype(BF16)
            kt = buf[0, 0, h].astype(BF16)
            vt = buf[0, 1, h].astype(BF16)
            s_b = jnp.where(ok_b, jnp.dot(qh, kt, preferred_element_type=F32), NEG)
            s_n = jnp.where(ok_n, lax.dot_general(qh, kn[:, sl].astype(BF16), nt, preferred_element_type=F32), NEG)
            m = jnp.maximum(jnp.max(s_b, axis=-1, keepdims=True), jnp.max(s_n, axis=-1, keepdims=True))
            p_b = jnp.exp(s_b - m)
            p_n = jnp.exp(s_n - m)
            den = jnp.sum(p_b, axis=-1, keepdims=True) + jnp.sum(p_n, axis=-1, keepdims=True)
            num = (lax.dot_general(p_b.astype(BF16), vt, nt, preferred_element_type=F32)
                   + jnp.dot(p_n.astype(BF16), vn[:, sl].astype(BF16), preferred_element_type=F32))
            o_parts.append(num / den)
            l_parts.append(jnp.broadcast_to(m + jnp.log(den), (T, ATT_DH)))
        outs[2 * gi][0] = jnp.concatenate(o_parts, axis=-1)
        outs[2 * gi + 1][0] = jnp.concatenate(l_parts, axis=-1)


def _datt_sample(z3, bufs_t, layer):
    b, t, _ = z3.shape
    w = ATT_HEADS * ATT_DH
    ng = len(ATT_PATTERNS)
    col = lambda c: pl.BlockSpec((1, t, w), lambda i: (i, 0, c))
    in_specs = ([col(COL_QA + gi) for gi in range(ng)] + [col(COL_KA + gi) for gi in range(ng)]
                + [col(COL_VA + gi) for gi in range(ng)]
                + [pl.BlockSpec((None, 1) + bt.shape[2:], lambda i: (layer, i, 0, 0, 0, 0)) for bt in bufs_t])
    res = pl.pallas_call(
        functools.partial(_satt_kernel, T=t),
        grid=(b,),
        in_specs=in_specs,
        out_specs=[pl.BlockSpec((1, t, w), lambda i: (i, 0, 0))] * (2 * ng),
        out_shape=[jax.ShapeDtypeStruct((b, t, w), F32)] * (2 * ng),
        compiler_params=_cparams(("arbitrary",)),
        name="datt_sample",
    )(*([z3] * (3 * ng)), *bufs_t)
    return [(res[2 * gi].reshape(b * t, w), res[2 * gi + 1].reshape(b * t, w)) for gi in range(ng)]


def _mix_kernel(x_ref, ga_ref, yr_ref, p_ref, o1_ref, l1_ref, o2_ref, l2_ref, o3_ref, l3_ref,
                gr_ref, gp_ref, gatt_ref, wr_ref, wp_ref, ps_ref, wa_ref, wo_ref, out_ref):
    y_r = jnp.dot(yr_ref[...].astype(BF16), wr_ref[...], preferred_element_type=F32)
    p = p_ref[...]
    parts = []
    for gi in range(len(POOL_WINDOWS)):
        parts.append(jnp.dot(p[:, gi * POOL_GW:(gi + 1) * POOL_GW].astype(BF16), wp_ref[gi],
                             preferred_element_type=F32))
    y_p = jnp.concatenate(parts, axis=-1) * ps_ref[...]
    l1, l2, l3 = l1_ref[...], l2_ref[...], l3_ref[...]
    lm = jnp.maximum(jnp.maximum(l1, l2), l3)
    w1, w2, w3 = jnp.exp(l1 - lm), jnp.exp(l2 - lm), jnp.exp(l3 - lm)
    att = (w1 * o1_ref[...] + w2 * o2_ref[...] + w3 * o3_ref[...]) / (w1 + w2 + w3)
    y_a = jnp.dot(att.astype(BF16), wa_ref[...], preferred_element_type=F32)
    mix = (jax.nn.sigmoid(gr_ref[...]) * y_r + jax.nn.sigmoid(gp_ref[...]) * y_p
           + jax.nn.sigmoid(gatt_ref[...]) * y_a)
    out = jnp.dot(mix.astype(BF16), wo_ref[...], preferred_element_type=F32)
    out_ref[...] = x_ref[...] + ga_ref[0] * out


def _mix(x, ga, z, yr, p, att_parts, w_ret_out, pool_w, pool_scale, w_att_out, w_out, per_row, rows_per_batch):
    m = x.shape[0]
    tm = 256
    tpb = max(rows_per_batch // tm, 1)
    row = lambda width: pl.BlockSpec((tm, width), lambda i: (i, 0))
    full = lambda shape: pl.BlockSpec(shape, lambda i: (0,) * len(shape))
    gate = lambda k: pl.BlockSpec((tm, D), lambda i: (i, k))
    (o1, l1), (o2, l2), (o3, l3) = att_parts
    return pl.pallas_call(
        _mix_kernel,
        grid=(m // tm,),
        in_specs=[row(D), _mod_spec(per_row, tm, tpb, 0), row(512), row(512),
                  row(256), row(256), row(256), row(256), row(256), row(256),
                  gate(0), gate(1), gate(2),
                  full((512, D)), full((4, POOL_GW, 256)), full((1, D)), full((256, D)), full((D, D))],
        out_specs=row(D),
        out_shape=jax.ShapeDtypeStruct((m, D), F32),
        compiler_params=_cparams(("arbitrary",)),
        name="mix",
    )(x, ga, yr, p, o1, l1, o2, l2, o3, l3, z, z, z, w_ret_out, pool_w, pool_scale, w_att_out, w_out)


def _split_bf16(x):
    hi = x.astype(BF16)
    return hi, (x - hi.astype(F32)).astype(BF16)


def _dot3(a_hi, a_lo, b_hi, b_lo):
    d = functools.partial(jnp.dot, preferred_element_type=F32)
    return d(a_hi, b_hi) + (d(a_hi, b_lo) + d(a_lo, b_hi))


def _sorting_network(n):
    pairs, p = [], 1
    while p < n:
        k = p
        while k >= 1:
            for j in range(k % p, n - k, 2 * k):
                for i in range(min(k, n - j - k)):
                    if (i + j) // (2 * p) == (i + j + k) // (2 * p):
                        pairs.append((i + j, i + j + k))
            k //= 2
        p *= 2
    return tuple(pairs)


_SORT16 = _sorting_network(PEER_TOPK)


def _extract_top(s, v_scr, tm):
    nt8 = PEER_NKEYS // 8
    for c in range(tm // 128):
        cs = slice(c * 128, (c + 1) * 128)
        a = [s[8 * j:8 * (j + 1), cs] for j in range(nt8)]
        for i, j in _SORT16:
            a[i], a[j] = jnp.maximum(a[i], a[j]), jnp.minimum(a[i], a[j])
        for k in range(PEER_TOPK):
            mx = jnp.max(a[0], axis=0, keepdims=True)
            v_scr[k:k + 1, cs] = mx
            pop = a[0] == mx
            last = PEER_TOPK - 1 - k
            for j in range(last):
                a[j] = jnp.where(pop, a[j + 1], a[j])
            a[last] = jnp.where(pop, NEG, a[last])


def _route_kernel(x_ref, g_ref, sh_ref, sc_ref, wqh_ref, wql_ref, k1h_ref, k1l_ref, k2h_ref, k2l_ref,
                  ht_ref, e1_ref, th_ref, e2_ref, v1_scr, v2_scr, *, tm):
    h2 = _rms_mod(x_ref[...], g_ref[...], sh_ref[0], sc_ref[0])
    ht = h2.T
    ht_hi, ht_lo = _split_bf16(ht)
    ht_ref[0] = ht_hi
    qt = _dot3(wqh_ref[...], wql_ref[...], ht_hi, ht_lo)
    half = PEER_NKEYS // 2
    sub = lax.broadcasted_iota(jnp.int32, (8, tm), 0)
    none = 2.0
    for h in range(PEER_HEADS):
        base = h * PEER_NKEYS
        q1h, q1l = _split_bf16(qt[base:base + half])
        q2h, q2l = _split_bf16(qt[base + half:base + PEER_NKEYS])
        s1 = _dot3(k1h_ref[h], k1l_ref[h], q1h, q1l)
        s2 = _dot3(k2h_ref[h], k2l_ref[h], q2h, q2l)
        e1 = jnp.exp(s1 - jnp.max(s1, axis=0, keepdims=True))
        e2 = jnp.exp(s2 - jnp.max(s2, axis=0, keepdims=True))
        _extract_top(e1, v1_scr, tm)
        _extract_top(e2, v2_scr, tm)
        ev1 = v1_scr[...]
        ev2 = v2_scr[...]
        blocks = [ev1[0:1] * ev2, ev1[1:2] * ev2[0:8]]
        for a in range(2, 8):
            blocks.append(jnp.where(sub < PEER_TOPK // (a + 1), ev1[a:a + 1] * ev2[0:8], -1.0))
        blocks.append(ev1[8:16] * ev2[0:1])
        cand = jnp.concatenate(blocks, axis=0)
        rem = cand
        thr = None
        for _ in range(PEER_TOPK):
            thr = jnp.max(rem, axis=0, keepdims=True)
            rem = jnp.where(rem == thr, -1.0, rem)
        sel = cand >= thr
        z = jnp.sum(jnp.where(sel, cand, 0.0), axis=0, keepdims=True)
        rz = 1.0 / z
        low = jnp.where(sel, jnp.concatenate([ev2] + [ev2[0:8]] * 7 + [jnp.broadcast_to(ev2[0:1], (8, tm))], axis=0),
                        none)
        th_a = [jnp.min(low[0:16], axis=0, keepdims=True)]
        for a in range(1, 8):
            th_a.append(jnp.min(low[8 + 8 * a:16 + 8 * a], axis=0, keepdims=True))
        for a in range(8, 16):
            th_a.append(low[72 + a - 8:73 + a - 8])
        thmap = jnp.full((PEER_NKEYS, tm), none, F32)
        for a in range(PEER_TOPK):
            thmap = jnp.where(e1 == ev1[a:a + 1], th_a[a], thmap)
        rows = slice(base, base + PEER_NKEYS)
        for ref, val in ((e1_ref, e1 * rz), (th_ref, thmap), (e2_ref, e2)):
            ref[0, rows, 0:tm] = val
            ref[0, rows, tm:] = jnp.zeros((PEER_NKEYS, LANE_PAD), F32)


def _route(x, g, sh, sc, lw, per_row, rows_per_batch, tm):
    m = x.shape[0]
    nt = m // tm
    tpb = max(rows_per_batch // tm, 1)
    hk = PEER_HEADS * PEER_NKEYS
    full = lambda a: pl.BlockSpec(a.shape, lambda i: (0,) * a.ndim)
    tile = lambda r: pl.BlockSpec((1, r, tm), lambda i: (i, 0, 0))
    ws = [lw["wq_hi"], lw["wq_lo"], lw["k1_hi"], lw["k1_lo"], lw["k2_hi"], lw["k2_lo"]]
    return pl.pallas_call(
        functools.partial(_route_kernel, tm=tm),
        grid=(nt,),
        in_specs=[pl.BlockSpec((tm, D), lambda i: (i, 0)), full(g),
                  _mod_spec(per_row, tm, tpb, 0), _mod_spec(per_row, tm, tpb, 0)] + [full(a) for a in ws],
        out_specs=[tile(D)] + [pl.BlockSpec((1, hk, tm + LANE_PAD), lambda i: (i, 0, 0))] * 3,
        out_shape=[jax.ShapeDtypeStruct((nt, D, tm), BF16)]
        + [jax.ShapeDtypeStruct((nt, hk, tm + LANE_PAD), F32)] * 3,
        scratch_shapes=[pltpu.VMEM((PEER_TOPK, tm), F32), pltpu.VMEM((PEER_TOPK, tm), F32)],
        compiler_params=_cparams(("arbitrary",)),
        name="peer_route",
    )(x, g, sh, sc, *ws)


def _gelu_tanh(x):
    return 0.5 * x * (1.0 + jnp.tanh(0.7978845608028654 * (x + 0.044715 * (x * x * x))))


def _experts_kernel(ht_ref, e1_ref, th_ref, e2_ref, u_ref, vt_ref, x_ref, ga_ref, fg_ref,
                    out_ref, acc_ref, h_ref, *, tm, final):
    e = pl.program_id(1)
    te = PEER_TILE
    lc = 128
    n_i1 = te // PEER_NKEYS

    @pl.when(e == 0)
    def _():
        acc_ref[...] = jnp.zeros_like(acc_ref)

    a_t = jnp.dot(u_ref[...], ht_ref[0], preferred_element_type=F32)
    for r in range(n_i1):
        for c in range(tm // lc):
            cs = slice(c * lc, (c + 1) * lc)
            w = None
            for h in range(PEER_HEADS):
                start = pl.multiple_of(h * PEER_NKEYS + e * n_i1, n_i1)
                e1row = e1_ref[0, pl.ds(start, n_i1), cs][r:r + 1]
                throw = th_ref[0, pl.ds(start, n_i1), cs][r:r + 1]
                e2 = e2_ref[0, h * PEER_NKEYS:(h + 1) * PEER_NKEYS, cs]
                term = jnp.where(e2 >= throw, e1row * e2, 0.0)
                w = term if w is None else w + term
            rs = slice(r * PEER_NKEYS, (r + 1) * PEER_NKEYS)
            h_ref[rs, cs] = (w * _gelu_tanh(a_t[rs, cs])).astype(BF16)
    acc_ref[...] += jnp.dot(vt_ref[0], h_ref[...], preferred_element_type=F32)

    @pl.when(e == pl.num_programs(1) - 1)
    def _():
        y = x_ref[...] + ga_ref[0] * acc_ref[...].T
        if final:
            ms = jnp.mean(y * y, axis=-1, keepdims=True)
            y = y * lax.rsqrt(ms + RMS_EPS) * fg_ref[...]
        out_ref[...] = y


def _experts(ht, e1, th, e2, u_bf, vt_bf, x, ga, fg, per_row, rows_per_batch, tm, final):
    m = x.shape[0]
    nt = m // tm
    te = PEER_TILE
    ne = PEER_EXPERTS // te
    tpb = max(rows_per_batch // tm, 1)
    hk = PEER_HEADS * PEER_NKEYS
    sel = pl.BlockSpec((1, hk, tm + LANE_PAD), lambda i, e: (i, 0, 0))
    return pl.pallas_call(
        functools.partial(_experts_kernel, tm=tm, final=final),
        grid=(nt, ne),
        in_specs=[pl.BlockSpec((1, D, tm), lambda i, e: (i, 0, 0)), sel, sel, sel,
                  pl.BlockSpec((te, D), lambda i, e: (e, 0)),
                  pl.BlockSpec((1, D, te), lambda i, e: (e, 0, 0)),
                  pl.BlockSpec((tm, D), lambda i, e: (i, 0)),
                  _mod_spec(per_row, tm, tpb, 0),
                  pl.BlockSpec((1, D), lambda i, e: (0, 0))],
        out_specs=pl.BlockSpec((tm, D), lambda i, e: (i, 0)),
        out_shape=jax.ShapeDtypeStruct((m, D), F32),
        scratch_shapes=[pltpu.VMEM((D, tm), F32), pltpu.VMEM((te, tm), BF16)],
        compiler_params=_cparams(("arbitrary", "arbitrary")),
        name="peer_experts",
    )(ht, e1, th, e2, u_bf, vt_bf, x, ga, fg)


def _group_rows(per_row, vec, rows_per_batch, tm):
    if per_row:
        rows = jnp.repeat(vec, rows_per_batch, axis=0)
        return rows.reshape(rows.shape[0] // tm, tm, D)
    return vec[:, None, :]


def _trunk_layer(x, mods, lw, per_row, b, t, ret_st0, pool_prev, bufs, layer, pos, final, final_g):
    sh1, sc1, ga1, sh2, sc2, ga2 = mods
    m = b * t
    gm = lambda v, tm: _group_rows(per_row, v, t, tm)
    z = _inproj(x, lw["g_mix"], gm(sh1, 512), gm(sc1, 512), lw["w_in"], per_row, t)
    z3 = z.reshape(b, t, IN_WIDTH)

    cos2, sin2 = _rope_tables(pos)
    chunk = RET_CHUNK if t >= RET_CHUNK else t
    yr, new_ret = _retention(z3, ret_st0, cos2, sin2, lw["gn_g"], lw["gn_b"], chunk)

    if per_row:
        p = _pool(z3, pool_prev, t, PAST_LEN, False)
        att_parts = _datt_sample(z3, bufs, layer)
    else:
        p = _pool(z3, None, 512, 0, True)
        att_parts = [_datt_prompt(z3, gi, dil) for gi, (win, dil) in enumerate(ATT_PATTERNS)]

    x1 = _mix(x, gm(ga1, 256), z, yr.reshape(m, 512), p.reshape(m, 512), att_parts,
              lw["w_ret_out"], lw["pool_w"], lw["pool_scale"], lw["w_att_out"], lw["w_out"], per_row, t)

    tm = 512
    ht, e1, th, e2 = _route(x1, lw["g_ffn"], gm(sh2, tm), gm(sc2, tm), lw, per_row, t, tm)
    x2 = _experts(ht, e1, th, e2, lw["u"], lw["v_t"], x1, gm(ga2, tm), final_g, per_row, t, tm, final)

    new_kv = []
    for gi, (win, dil) in enumerate(ATT_PATTERNS):
        keep = t if per_row else min(win, t)
        kk = z3[:, t - keep:, (COL_KA + gi) * CB:(COL_KA + gi + 1) * CB].reshape(b, keep, ATT_HEADS, ATT_DH)
        vv = z3[:, t - keep:, (COL_VA + gi) * CB:(COL_VA + gi + 1) * CB].reshape(b, keep, ATT_HEADS, ATT_DH)
        new_kv.append(jnp.stack([kk, vv], axis=2))
    up = z3[:, :, COL_UP * CB:COL_UP * CB + 512]
    if per_row:
        new_pool = jnp.concatenate([pool_prev[:, 1:], up], axis=1)[:, -(POOL_HALO - 1):]
    else:
        new_pool = up[:, t - (POOL_HALO - 1):]
    return x2, (new_ret, new_pool, new_kv)


def kernel(x_prompt, x_sample, state_ret, state_pool, cache_kv_w128, cache_kv_w512, cache_kv_w2048,
           c_prompt, c_sample, ada_w, ada_b, norm_mix_g, norm_ffn_g, w_in, ret_gn_g, ret_gn_b,
           w_ret_out, pool_w, pool_scale, w_att_out, w_out, peer_wq, peer_keys, peer_u, peer_v,
           final_norm_g):
    depth = ada_w.shape[0]
    bp, s, _ = x_prompt.shape
    bs, t, _ = x_sample.shape
    caches_t = tuple(jnp.transpose(c, (0, 1, 3, 4, 5, 2)) for c in (cache_kv_w128, cache_kv_w512, cache_kv_w2048))

    mods_all = _adaln(jnp.concatenate([c_prompt, c_sample], axis=0), ada_w, ada_b)

    xp = x_prompt.reshape(bp * s, D)
    xs = x_sample.reshape(bs * t, D)
    final_g = final_norm_g.reshape(1, D)
    zero_state = jnp.zeros((bp, RET_HEADS, RET_DK, RET_DV), F32)
    outs_p, outs_s = [], []
    for l in range(depth):
        n_gate = 3 * D
        w_in_l = jnp.concatenate([w_in[l][:, IN_WIDTH - n_gate:], w_in[l][:, :IN_WIDTH - n_gate]], axis=1)
        wq_hi, wq_lo = _split_bf16(peer_wq[l].T)
        k1_hi, k1_lo = _split_bf16(peer_keys[l][:, 0])
        k2_hi, k2_lo = _split_bf16(peer_keys[l][:, 1])
        lw = {
            "g_mix": norm_mix_g[l].reshape(1, D), "g_ffn": norm_ffn_g[l].reshape(1, D),
            "w_in": w_in_l.astype(BF16),
            "gn_g": ret_gn_g[l].reshape(1, -1), "gn_b": ret_gn_b[l].reshape(1, -1),
            "w_ret_out": w_ret_out[l].astype(BF16), "pool_w": pool_w[l].astype(BF16),
            "pool_scale": pool_scale[l].reshape(1, D), "w_att_out": w_att_out[l].astype(BF16),
            "w_out": w_out[l].astype(BF16),
            "wq_hi": wq_hi, "wq_lo": wq_lo, "k1_hi": k1_hi, "k1_lo": k1_lo, "k2_hi": k2_hi, "k2_lo": k2_lo,
            "u": peer_u[l].astype(BF16),
            "v_t": peer_v[l].reshape(-1, PEER_TILE, D).transpose(0, 2, 1).astype(BF16),
        }
        ml = mods_all[l]
        mods_p = tuple(ml[:bp, k * D:(k + 1) * D] for k in range(6))
        mods_s = tuple(ml[bp:, k * D:(k + 1) * D] for k in range(6))
        final = l == depth - 1
        xp, st_p = _trunk_layer(xp, mods_p, lw, False, bp, s, zero_state, None, None, l,
                                jnp.arange(s), final, final_g)
        pool_prev = jnp.pad(state_pool[l], ((0, 0), (1, 0), (0, 0)))
        xs, st_s = _trunk_layer(xs, mods_s, lw, True, bs, t, state_ret[l], pool_prev, caches_t, l,
                                PAST_LEN + jnp.arange(t), final, final_g)
        outs_p.append(st_p)
        outs_s.append(st_s)

    def stack(outs, pick):
        return jnp.stack([pick(o) for o in outs])

    res = [xp.reshape(bp, s, D), xs.reshape(bs, t, D),
           stack(outs_p, lambda o: o[0]), stack(outs_s, lambda o: o[0]),
           stack(outs_p, lambda o: o[1]), stack(outs_s, lambda o: o[1])]
    for gi in range(len(ATT_PATTERNS)):
        res.append(stack(outs_p, lambda o: o[2][gi]))
        res.append(stack(outs_s, lambda o: o[2][gi]))
    return tuple(res)
```

```python
import functools
import math

import jax
import jax.numpy as jnp
from jax import lax
from jax.experimental import pallas as pl
from jax.experimental.pallas import tpu as pltpu

F32 = jnp.float32
BF16 = jnp.bfloat16

D = 1024
RMS_EPS = 1e-6
GN_EPS = 1e-5
PAST_LEN = 2048

RET_HEADS = 4
RET_DK = 64
RET_DV = 128
RET_CHUNK = 128
RET_SEQS_LONG = 2
RET_SEQS_SHORT = 8
ROPE_BASE = 10000.0
RET_LOG_GAMMA = tuple(math.log(1.0 - 2.0 ** (-5.0 - h)) for h in range(RET_HEADS))

POOL_WINDOWS = (2, 4, 8, 16)
POOL_GW = 128
POOL_HALO = 16

ATT_PATTERNS = ((128, 1), (512, 4), (2048, 16))
ATT_HEADS = 4
ATT_DH = 64
ATT_BLOCK = 128
ATT_PAIR = 2
ATT_UNROLL = 4
ATT_BACK = 128
NEG = -1e30

PEER_HEADS = 8
PEER_NKEYS = 128
PEER_EXPERTS = PEER_NKEYS * PEER_NKEYS
PEER_TOPK = 16
PEER_TILE = 8 * PEER_NKEYS
LANE_PAD = 128

IN_WIDTH = 7424
CB = 256
COL_GATE = 0
COL_QR, COL_KR, COL_VR, COL_GR, COL_UP = 12, 13, 14, 16, 18
COL_QA, COL_KA, COL_VA = 20, 23, 26
N_CB = IN_WIDTH // CB

VMEM_LIMIT = 60 * 1024 * 1024


def _cparams(sem):
    return pltpu.CompilerParams(dimension_semantics=sem, vmem_limit_bytes=VMEM_LIMIT)


def _mod_spec(per_row, tm, tiles_per_batch, grid_pos):
    if per_row:
        return pl.BlockSpec((1, tm, D), lambda *g: (g[grid_pos], 0, 0))
    return pl.BlockSpec((1, 1, D), lambda *g: (g[grid_pos] // tiles_per_batch, 0, 0))


def _rms_mod(x, g, sh, sc):
    ms = jnp.mean(x * x, axis=-1, keepdims=True)
    h = x * lax.rsqrt(ms + RMS_EPS) * g
    return h * (1.0 + sc) + sh


def _ada_kernel(c_ref, w_ref, b_ref, o_ref):
    c = c_ref[...]
    s = c * jax.nn.sigmoid(c)
    o_ref[0] = jnp.dot(s.astype(BF16), w_ref[0].astype(BF16), preferred_element_type=F32) + b_ref[0]


def _adaln(c_all, ada_w, ada_b):
    depth, _, n = ada_w.shape
    m = c_all.shape[0]
    tn = 512
    return pl.pallas_call(
        _ada_kernel,
        grid=(depth, n // tn),
        in_specs=[pl.BlockSpec((m, D), lambda l, j: (0, 0)),
                  pl.BlockSpec((1, D, tn), lambda l, j: (l, 0, j)),
                  pl.BlockSpec((1, 1, tn), lambda l, j: (l, 0, j))],
        out_specs=pl.BlockSpec((1, m, tn), lambda l, j: (l, 0, j)),
        out_shape=jax.ShapeDtypeStruct((depth, m, n), F32),
        compiler_params=_cparams(("arbitrary", "arbitrary")),
        name="adaln",
    )(c_all, ada_w, ada_b.reshape(depth, 1, n))


def _inproj_kernel(x_ref, g_ref, sh_ref, sc_ref, w_ref, o_ref):
    h = _rms_mod(x_ref[...], g_ref[...], sh_ref[0], sc_ref[0])
    o_ref[...] = jnp.dot(h.astype(BF16), w_ref[...], preferred_element_type=F32)


def _inproj(x, g, sh, sc, w_bf, per_row, rows_per_batch):
    m = x.shape[0]
    tm = 512
    tn = IN_WIDTH // 2
    tpb = max(rows_per_batch // tm, 1)
    return pl.pallas_call(
        _inproj_kernel,
        grid=(2, m // tm),
        in_specs=[pl.BlockSpec((tm, D), lambda j, i: (i, 0)),
                  pl.BlockSpec((1, D), lambda j, i: (0, 0)),
                  _mod_spec(per_row, tm, tpb, 1),
                  _mod_spec(per_row, tm, tpb, 1),
                  pl.BlockSpec((D, tn), lambda j, i: (0, j))],
        out_specs=pl.BlockSpec((tm, tn), lambda j, i: (i, j)),
        out_shape=jax.ShapeDtypeStruct((m, IN_WIDTH), F32),
        compiler_params=_cparams(("arbitrary", "arbitrary")),
        name="inproj",
    )(x, g, sh, sc, w_bf)


def _ret_kernel(q_ref, k_ref, v_ref, g_ref, cos_ref, sin_ref, st0_ref, gng_ref, gnb_ref,
                y_ref, st_ref, st_scr, *, C, nb):
    c = pl.program_id(1)

    @pl.when(c == 0)
    def _():
        st_scr[...] = st0_ref[...]

    cos = cos_ref[...]
    sin = sin_ref[...]
    lane = lax.broadcasted_iota(jnp.int32, (C, RET_HEADS * RET_DK), 1)
    first_half = (lane % RET_DK) < (RET_DK // 2)

    def rot(x):
        nl = RET_HEADS * RET_DK
        swapped = jnp.where(first_half, pltpu.roll(x, nl - RET_DK // 2, 1), pltpu.roll(x, RET_DK // 2, 1))
        return x * cos + swapped * sin

    ii = lax.broadcasted_iota(jnp.int32, (C, C), 0)
    jj = lax.broadcasted_iota(jnp.int32, (C, C), 1)
    diff = (ii - jj).astype(F32)
    causal = ii >= jj
    row = lax.broadcasted_iota(jnp.int32, (C, 1), 0).astype(F32)
    for bi in range(nb):
        q = rot(q_ref[bi])
        k = rot(k_ref[bi]) * (RET_DK ** -0.5)
        v = v_ref[bi]
        g = g_ref[bi]
        for h in range(RET_HEADS):
            lg = RET_LOG_GAMMA[h]
            qh = q[:, h * RET_DK:(h + 1) * RET_DK].astype(BF16)
            khf = k[:, h * RET_DK:(h + 1) * RET_DK]
            vh = v[:, h * RET_DV:(h + 1) * RET_DV].astype(BF16)
            st = st_scr[bi, h]
            decay = jnp.where(causal, jnp.exp(lg * jnp.maximum(diff, 0.0)), 0.0)
            s = lax.dot_general(qh, khf.astype(BF16), (((1,), (1,)), ((), ())), preferred_element_type=F32) * decay
            o = jnp.dot(s.astype(BF16), vh, preferred_element_type=F32)
            o = o + jnp.dot(qh, st.astype(BF16), preferred_element_type=F32) * jnp.exp(lg * (row + 1.0))
            kdec = (khf * jnp.exp(lg * (C - 1.0 - row))).astype(BF16)
            st_scr[bi, h] = math.exp(lg * C) * st + lax.dot_general(
                kdec, vh, (((0,), (0,)), ((), ())), preferred_element_type=F32)
            mu = jnp.mean(o, axis=-1, keepdims=True)
            var = jnp.mean(jnp.square(o - mu), axis=-1, keepdims=True)
            on = (o - mu) * lax.rsqrt(var + GN_EPS)
            on = on * gng_ref[:, h * RET_DV:(h + 1) * RET_DV] + gnb_ref[:, h * RET_DV:(h + 1) * RET_DV]
            gh = g[:, h * RET_DV:(h + 1) * RET_DV]
            y_ref[bi, :, h * RET_DV:(h + 1) * RET_DV] = gh * jax.nn.sigmoid(gh) * on

    @pl.when(c == pl.num_programs(1) - 1)
    def _():
        st_ref[...] = st_scr[...]


def _retention(z3, st0, cos2, sin2, gn_g, gn_b, C):
    b, t, _ = z3.shape
    nc = t // C
    nb = RET_SEQS_LONG if C == RET_CHUNK else RET_SEQS_SHORT
    vq = RET_HEADS * RET_DK
    vv = RET_HEADS * RET_DV
    return pl.pallas_call(
        functools.partial(_ret_kernel, C=C, nb=nb),
        grid=(b // nb, nc),
        in_specs=[pl.BlockSpec((nb, C, vq), lambda i, c: (i, c, COL_QR)),
                  pl.BlockSpec((nb, C, vq), lambda i, c: (i, c, COL_KR)),
                  pl.BlockSpec((nb, C, vv), lambda i, c: (i, c, COL_VR // 2)),
                  pl.BlockSpec((nb, C, vv), lambda i, c: (i, c, COL_GR // 2)),
                  pl.BlockSpec((C, vq), lambda i, c: (c, 0)),
                  pl.BlockSpec((C, vq), lambda i, c: (c, 0)),
                  pl.BlockSpec((nb, RET_HEADS, RET_DK, RET_DV), lambda i, c: (i, 0, 0, 0)),
                  pl.BlockSpec((1, vv), lambda i, c: (0, 0)),
                  pl.BlockSpec((1, vv), lambda i, c: (0, 0))],
        out_specs=[pl.BlockSpec((nb, C, vv), lambda i, c: (i, c, 0)),
                   pl.BlockSpec((nb, RET_HEADS, RET_DK, RET_DV), lambda i, c: (i, 0, 0, 0))],
        out_shape=[jax.ShapeDtypeStruct((b, t, vv), F32),
                   jax.ShapeDtypeStruct((b, RET_HEADS, RET_DK, RET_DV), F32)],
        scratch_shapes=[pltpu.VMEM((nb, RET_HEADS, RET_DK, RET_DV), F32)],
        compiler_params=_cparams(("arbitrary", "arbitrary")),
        name="retention",
    )(z3, z3, z3, z3, cos2, sin2, st0, gn_g, gn_b)


def _rope_tables(pos):
    half = RET_DK // 2
    inv = 1.0 / (ROPE_BASE ** jnp.linspace(0.0, 1.0, half, dtype=F32))
    ang = pos.astype(F32)[:, None] * inv[None, :]
    cos, sin = jnp.cos(ang), jnp.sin(ang)
    cos2 = jnp.tile(jnp.concatenate([cos, cos], axis=-1), (1, RET_HEADS))
    sin2 = jnp.tile(jnp.concatenate([-sin, sin], axis=-1), (1, RET_HEADS))
    return cos2, sin2


def _pool_kernel(prev_ref, cur_ref, p_ref, *, tm, pos0, zero_first):
    i = pl.program_id(1)
    cur = cur_ref[0]
    prev = prev_ref[0]
    if zero_first:
        prev = jnp.where(i == 0, 0.0, prev)
    ext = jnp.concatenate([prev, cur], axis=0)
    row = lax.broadcasted_iota(jnp.int32, (tm, 1), 0) + i * tm + pos0
    acc = ext
    sums = []
    for s in (1, 2, 4, 8):
        acc = acc + pltpu.roll(acc, s, 0)
        sums.append(acc)
    for gi, w in enumerate(POOL_WINDOWS):
        sl = slice(gi * POOL_GW, (gi + 1) * POOL_GW)
        win = sums[gi][POOL_HALO:, sl]
        cnt = jnp.minimum(row + 1, w).astype(F32)
        p_ref[0, :, sl] = win / cnt - cur[:, sl]


def _pool(z3, prev, tm, pos0, zero_first):
    b, t, _ = z3.shape
    nt = t // tm
    width = 4 * POOL_GW
    if prev is None:
        prev_arr = z3
        hb = tm // POOL_HALO
        prev_spec = pl.BlockSpec((1, POOL_HALO, width), lambda bi, i: (bi, jnp.maximum(i * hb - 1, 0), COL_UP // 2))
    else:
        prev_arr = prev
        prev_spec = pl.BlockSpec((1, POOL_HALO, width), lambda bi, i: (bi, 0, 0))
    return pl.pallas_call(
        functools.partial(_pool_kernel, tm=tm, pos0=pos0, zero_first=zero_first),
        grid=(b, nt),
        in_specs=[prev_spec, pl.BlockSpec((1, tm, width), lambda bi, i: (bi, i, COL_UP // 2))],
        out_specs=pl.BlockSpec((1, tm, width), lambda bi, i: (bi, i, 0)),
        out_shape=jax.ShapeDtypeStruct((b, t, width), F32),
        compiler_params=_cparams(("arbitrary", "arbitrary")),
        name="pool",
    )(prev_arr, z3)


def _datt_kernel(q_ref, kc_ref, kp_ref, vc_ref, vp_ref, o_ref, l_ref, *, dil):
    n = pl.program_id(1)
    qb = ATT_BLOCK
    ii = lax.broadcasted_iota(jnp.int32, (qb, qb), 0)
    jj = lax.broadcasted_iota(jnp.int32, (qb, qb), 1)
    mask_c = jj <= ii
    mask_p = (jj >= ii) & (n > 0)
    nt = (((1,), (1,)), ((), ()))

    def residue(r, carry):
        rows = pl.ds(r, qb, stride=dil) if dil > 1 else pl.ds(0, qb)
        q = q_ref[0, rows, :] * (ATT_DH ** -0.5)
        kc, kp, vc, vp = kc_ref[0, rows, :], kp_ref[0, rows, :], vc_ref[0, rows, :], vp_ref[0, rows, :]
        outs, lses = [], []
        for h in range(q_ref.shape[-1] // ATT_DH):
            sl = slice(h * ATT_DH, (h + 1) * ATT_DH)
            qh = q[:, sl].astype(BF16)
            s_c = jnp.where(mask_c, lax.dot_general(qh, kc[:, sl].astype(BF16), nt, preferred_element_type=F32), NEG)
            s_p = jnp.where(mask_p, lax.dot_general(qh, kp[:, sl].astype(BF16), nt, preferred_element_type=F32), NEG)
            m = jnp.maximum(jnp.max(s_c, axis=-1, keepdims=True), jnp.max(s_p, axis=-1, keepdims=True))
            p_c = jnp.exp(s_c - m)
            p_p = jnp.exp(s_p - m)
            den = jnp.sum(p_c, axis=-1, keepdims=True) + jnp.sum(p_p, axis=-1, keepdims=True)
            num = (jnp.dot(p_c.astype(BF16), vc[:, sl].astype(BF16), preferred_element_type=F32)
                   + jnp.dot(p_p.astype(BF16), vp[:, sl].astype(BF16), preferred_element_type=F32))
            outs.append(num / den)
            lses.append(jnp.broadcast_to(m + jnp.log(den), (qb, ATT_DH)))
        o_ref[0, rows, :] = jnp.concatenate(outs, axis=-1)
        l_ref[0, rows, :] = jnp.concatenate(lses, axis=-1)
        return carry

    if dil == 1:
        residue(0, 0)
    else:
        lax.fori_loop(0, dil, residue, 0, unroll=min(dil, ATT_UNROLL))


def _datt_prompt(z3, gi, dil):
    b, s, _ = z3.shape
    tb = ATT_BLOCK * dil
    w = ATT_HEADS * ATT_DH
    heads = ATT_HEADS if dil == 1 else ATT_PAIR
    wp = heads * ATT_DH
    npair = ATT_HEADS // heads

    def spec(col, prev):
        if prev:
            return pl.BlockSpec((1, tb, wp), lambda bi, n, hp: (bi, jnp.maximum(n - 1, 0), (col + gi) * npair + hp))
        return pl.BlockSpec((1, tb, wp), lambda bi, n, hp: (bi, n, (col + gi) * npair + hp))

    o, lse = pl.pallas_call(
        functools.partial(_datt_kernel, dil=dil),
        grid=(b, s // tb, npair),
        in_specs=[spec(COL_QA, False), spec(COL_KA, False), spec(COL_KA, True),
                  spec(COL_VA, False), spec(COL_VA, True)],
        out_specs=[pl.BlockSpec((1, tb, wp), lambda bi, n, hp: (bi, n, hp))] * 2,
        out_shape=[jax.ShapeDtypeStruct((b, s, w), F32)] * 2,
        compiler_params=_cparams(("arbitrary", "arbitrary", "arbitrary")),
        name=f"datt_prompt_d{dil}",
    )(z3, z3, z3, z3, z3)
    return o.reshape(b * s, w), lse.reshape(b * s, w)


def _satt_kernel(*refs, T):
    ng = len(ATT_PATTERNS)
    qkv = refs[:3 * ng]
    bufs = refs[3 * ng:4 * ng]
    outs = refs[4 * ng:]
    nt = (((1,), (1,)), ((), ()))
    for gi, (win, dil) in enumerate(ATT_PATTERNS):
        q = qkv[gi][0] * (ATT_DH ** -0.5)
        kn = qkv[ng + gi][0]
        vn = qkv[2 * ng + gi][0]
        buf = bufs[gi]
        Wb = buf.shape[-1]
        qi_b = lax.broadcasted_iota(jnp.int32, (T, Wb), 0)
        dist_b = Wb + qi_b - lax.broadcasted_iota(jnp.int32, (T, Wb), 1)
        ok_b = (dist_b % dil == 0) & (dist_b <= dil * ATT_BACK)
        dist_n = lax.broadcasted_iota(jnp.int32, (T, T), 0) - lax.broadcasted_iota(jnp.int32, (T, T), 1)
        ok_n = (dist_n >= 0) & (dist_n % dil == 0)
        o_parts, l_parts = [], []
        for h in range(ATT_HEADS):
            sl = slice(h * ATT_DH, (h + 1) * ATT_DH)
            qh = q[:, sl].astype(BF16)
            kt = buf[0, 0, h].astype(BF16)
            vt = buf[0, 1, h].astype(BF16)
            s_b = jnp.where(ok_b, jnp.dot(qh, kt, preferred_element_type=F32), NEG)
            s_n = jnp.where(ok_n, lax.dot_general(qh, kn[:, sl].astype(BF16), nt, preferred_element_type=F32), NEG)
            m = jnp.maximum(jnp.max(s_b, axis=-1, keepdims=True), jnp.max(s_n, axis=-1, keepdims=True))
            p_b = jnp.exp(s_b - m)
            p_n = jnp.exp(s_n - m)
            den = jnp.sum(p_b, axis=-1, keepdims=True) + jnp.sum(p_n, axis=-1, keepdims=True)
            num = (lax.dot_general(p_b.astype(BF16), vt, nt, preferred_element_type=F32)
                   + jnp.dot(p_n.astype(BF16), vn[:, sl].astype(BF16), preferred_element_type=F32))
            o_parts.append(num / den)
            l_parts.append(jnp.broadcast_to(m + jnp.log(den), (T, ATT_DH)))
        outs[2 * gi][0] = jnp.concatenate(o_parts, axis=-1)
        outs[2 * gi + 1][0] = jnp.concatenate(l_parts, axis=-1)


def _datt_sample(z3, bufs_t, layer):
    b, t, _ = z3.shape
    w = ATT_HEADS * ATT_DH
    ng = len(ATT_PATTERNS)
    col = lambda c: pl.BlockSpec((1, t, w), lambda i: (i, 0, c))
    in_specs = ([col(COL_QA + gi) for gi in range(ng)] + [col(COL_KA + gi) for gi in range(ng)]
                + [col(COL_VA + gi) for gi in range(ng)]
                + [pl.BlockSpec((None, 1) + bt.shape[2:], lambda i: (layer, i, 0, 0, 0, 0)) for bt in bufs_t])
    res = pl.pallas_call(
        functools.partial(_satt_kernel, T=t),
        grid=(b,),
        in_specs=in_specs,
        out_specs=[pl.BlockSpec((1, t, w), lambda i: (i, 0, 0))] * (2 * ng),
        out_shape=[jax.ShapeDtypeStruct((b, t, w), F32)] * (2 * ng),
        compiler_params=_cparams(("arbitrary",)),
        name="datt_sample",
    )(*([z3] * (3 * ng)), *bufs_t)
    return [(res[2 * gi].reshape(b * t, w), res[2 * gi + 1].reshape(b * t, w)) for gi in range(ng)]


def _mix_kernel(x_ref, ga_ref, yr_ref, p_ref, o1_ref, l1_ref, o2_ref, l2_ref, o3_ref, l3_ref,
                gr_ref, gp_ref, gatt_ref, wr_ref, wp_ref, ps_ref, wa_ref, wo_ref, out_ref):
    y_r = jnp.dot(yr_ref[...].astype(BF16), wr_ref[...], preferred_element_type=F32)
    p = p_ref[...]
    parts = []
    for gi in range(len(POOL_WINDOWS)):
        parts.append(jnp.dot(p[:, gi * POOL_GW:(gi + 1) * POOL_GW].astype(BF16), wp_ref[gi],
                             preferred_element_type=F32))
    y_p = jnp.concatenate(parts, axis=-1) * ps_ref[...]
    l1, l2, l3 = l1_ref[...], l2_ref[...], l3_ref[...]
    lm = jnp.maximum(jnp.maximum(l1, l2), l3)
    w1, w2, w3 = jnp.exp(l1 - lm), jnp.exp(l2 - lm), jnp.exp(l3 - lm)
    att = (w1 * o1_ref[...] + w2 * o2_ref[...] + w3 * o3_ref[...]) / (w1 + w2 + w3)
    y_a = jnp.dot(att.astype(BF16), wa_ref[...], preferred_element_type=F32)
    mix = (jax.nn.sigmoid(gr_ref[...]) * y_r + jax.nn.sigmoid(gp_ref[...]) * y_p
           + jax.nn.sigmoid(gatt_ref[...]) * y_a)
    out = jnp.dot(mix.astype(BF16), wo_ref[...], preferred_element_type=F32)
    out_ref[...] = x_ref[...] + ga_ref[0] * out


def _mix(x, ga, z, yr, p, att_parts, w_ret_out, pool_w, pool_scale, w_att_out, w_out, per_row, rows_per_batch):
    m = x.shape[0]
    tm = 256
    tpb = max(rows_per_batch // tm, 1)
    row = lambda width: pl.BlockSpec((tm, width), lambda i: (i, 0))
    full = lambda shape: pl.BlockSpec(shape, lambda i: (0,) * len(shape))
    gate = lambda k: pl.BlockSpec((tm, D), lambda i: (i, k))
    (o1, l1), (o2, l2), (o3, l3) = att_parts
    return pl.pallas_call(
        _mix_kernel,
        grid=(m // tm,),
        in_specs=[row(D), _mod_spec(per_row, tm, tpb, 0), row(512), row(512),
                  row(256), row(256), row(256), row(256), row(256), row(256),
                  gate(0), gate(1), gate(2),
                  full((512, D)), full((4, POOL_GW, 256)), full((1, D)), full((256, D)), full((D, D))],
        out_specs=row(D),
        out_shape=jax.ShapeDtypeStruct((m, D), F32),
        compiler_params=_cparams(("arbitrary",)),
        name="mix",
    )(x, ga, yr, p, o1, l1, o2, l2, o3, l3, z, z, z, w_ret_out, pool_w, pool_scale, w_att_out, w_out)


def _split_bf16(x):
    hi = x.astype(BF16)
    return hi, (x - hi.astype(F32)).astype(BF16)


def _dot3(a_hi, a_lo, b_hi, b_lo):
    d = functools.partial(jnp.dot, preferred_element_type=F32)
    return d(a_hi, b_hi) + (d(a_hi, b_lo) + d(a_lo, b_hi))


def _sorting_network(n):
    pairs, p = [], 1
    while p < n:
        k = p
        while k >= 1:
            for j in range(k % p, n - k, 2 * k):
                for i in range(min(k, n - j - k)):
                    if (i + j) // (2 * p) == (i + j + k) // (2 * p):
                        pairs.append((i + j, i + j + k))
            k //= 2
        p *= 2
    return tuple(pairs)


_SORT16 = _sorting_network(PEER_TOPK)


def _extract_top(s, v_scr, tm):
    nt8 = PEER_NKEYS // 8
    for c in range(tm // 128):
        cs = slice(c * 128, (c + 1) * 128)
        a = [s[8 * j:8 * (j + 1), cs] for j in range(nt8)]
        for i, j in _SORT16:
            a[i], a[j] = jnp.maximum(a[i], a[j]), jnp.minimum(a[i], a[j])
        for k in range(PEER_TOPK):
            mx = jnp.max(a[0], axis=0, keepdims=True)
            v_scr[k:k + 1, cs] = mx
            pop = a[0] == mx
            last = PEER_TOPK - 1 - k
            for j in range(last):
                a[j] = jnp.where(pop, a[j + 1], a[j])
            a[last] = jnp.where(pop, NEG, a[last])


def _route_kernel(x_ref, g_ref, sh_ref, sc_ref, wqh_ref, wql_ref, k1h_ref, k1l_ref, k2h_ref, k2l_ref,
                  ht_ref, e1_ref, th_ref, e2_ref, v1_scr, v2_scr, *, tm):
    h2 = _rms_mod(x_ref[...], g_ref[...], sh_ref[0], sc_ref[0])
    ht = h2.T
    ht_hi, ht_lo = _split_bf16(ht)
    ht_ref[0] = ht_hi
    qt = _dot3(wqh_ref[...], wql_ref[...], ht_hi, ht_lo)
    half = PEER_NKEYS // 2
    sub = lax.broadcasted_iota(jnp.int32, (8, tm), 0)
    none = 2.0
    for h in range(PEER_HEADS):
        base = h * PEER_NKEYS
        q1h, q1l = _split_bf16(qt[base:base + half])
        q2h, q2l = _split_bf16(qt[base + half:base + PEER_NKEYS])
        s1 = _dot3(k1h_ref[h], k1l_ref[h], q1h, q1l)
        s2 = _dot3(k2h_ref[h], k2l_ref[h], q2h, q2l)
        e1 = jnp.exp(s1 - jnp.max(s1, axis=0, keepdims=True))
        e2 = jnp.exp(s2 - jnp.max(s2, axis=0, keepdims=True))
        _extract_top(e1, v1_scr, tm)
        _extract_top(e2, v2_scr, tm)
        ev1 = v1_scr[...]
        ev2 = v2_scr[...]
        blocks = [ev1[0:1] * ev2, ev1[1:2] * ev2[0:8]]
        for a in range(2, 8):
            blocks.append(jnp.where(sub < PEER_TOPK // (a + 1), ev1[a:a + 1] * ev2[0:8], -1.0))
        blocks.append(ev1[8:16] * ev2[0:1])
        cand = jnp.concatenate(blocks, axis=0)
        rem = cand
        thr = None
        for _ in range(PEER_TOPK):
            thr = jnp.max(rem, axis=0, keepdims=True)
            rem = jnp.where(rem == thr, -1.0, rem)
        sel = cand >= thr
        z = jnp.sum(jnp.where(sel, cand, 0.0), axis=0, keepdims=True)
        rz = 1.0 / z
        low = jnp.where(sel, jnp.concatenate([ev2] + [ev2[0:8]] * 7 + [jnp.broadcast_to(ev2[0:1], (8, tm))], axis=0),
                        none)
        th_a = [jnp.min(low[0:16], axis=0, keepdims=True)]
        for a in range(1, 8):
            th_a.append(jnp.min(low[8 + 8 * a:16 + 8 * a], axis=0, keepdims=True))
        for a in range(8, 16):
            th_a.append(low[72 + a - 8:73 + a - 8])
        thmap = jnp.full((PEER_NKEYS, tm), none, F32)
        for a in range(PEER_TOPK):
            thmap = jnp.where(e1 == ev1[a:a + 1], th_a[a], thmap)
        rows = slice(base, base + PEER_NKEYS)
        for ref, val in ((e1_ref, e1 * rz), (th_ref, thmap), (e2_ref, e2)):
            ref[0, rows, 0:tm] = val
            ref[0, rows, tm:] = jnp.zeros((PEER_NKEYS, LANE_PAD), F32)


def _route(x, g, sh, sc, lw, per_row, rows_per_batch, tm):
    m = x.shape[0]
    nt = m // tm
    tpb = max(rows_per_batch // tm, 1)
    hk = PEER_HEADS * PEER_NKEYS
    full = lambda a: pl.BlockSpec(a.shape, lambda i: (0,) * a.ndim)
    tile = lambda r: pl.BlockSpec((1, r, tm), lambda i: (i, 0, 0))
    ws = [lw["wq_hi"], lw["wq_lo"], lw["k1_hi"], lw["k1_lo"], lw["k2_hi"], lw["k2_lo"]]
    return pl.pallas_call(
        functools.partial(_route_kernel, tm=tm),
        grid=(nt,),
        in_specs=[pl.BlockSpec((tm, D), lambda i: (i, 0)), full(g),
                  _mod_spec(per_row, tm, tpb, 0), _mod_spec(per_row, tm, tpb, 0)] + [full(a) for a in ws],
        out_specs=[tile(D)] + [pl.BlockSpec((1, hk, tm + LANE_PAD), lambda i: (i, 0, 0))] * 3,
        out_shape=[jax.ShapeDtypeStruct((nt, D, tm), BF16)]
        + [jax.ShapeDtypeStruct((nt, hk, tm + LANE_PAD), F32)] * 3,
        scratch_shapes=[pltpu.VMEM((PEER_TOPK, tm), F32), pltpu.VMEM((PEER_TOPK, tm), F32)],
        compiler_params=_cparams(("arbitrary",)),
        name="peer_route",
    )(x, g, sh, sc, *ws)


def _gelu_tanh(x):
    return 0.5 * x * (1.0 + jnp.tanh(0.7978845608028654 * (x + 0.044715 * (x * x * x))))


def _experts_kernel(ht_ref, e1_ref, th_ref, e2_ref, u_ref, vt_ref, x_ref, ga_ref, fg_ref,
                    out_ref, acc_ref, h_ref, *, tm, final):
    e = pl.program_id(1)
    te = PEER_TILE
    lc = 128
    n_i1 = te // PEER_NKEYS

    @pl.when(e == 0)
    def _():
        acc_ref[...] = jnp.zeros_like(acc_ref)

    a_t = jnp.dot(u_ref[...], ht_ref[0], preferred_element_type=F32)
    for r in range(n_i1):
        for c in range(tm // lc):
            cs = slice(c * lc, (c + 1) * lc)
            w = None
            for h in range(PEER_HEADS):
                start = pl.multiple_of(h * PEER_NKEYS + e * n_i1, n_i1)
                e1row = e1_ref[0, pl.ds(start, n_i1), cs][r:r + 1]
                throw = th_ref[0, pl.ds(start, n_i1), cs][r:r + 1]
                e2 = e2_ref[0, h * PEER_NKEYS:(h + 1) * PEER_NKEYS, cs]
                term = jnp.where(e2 >= throw, e1row * e2, 0.0)
                w = term if w is None else w + term
            rs = slice(r * PEER_NKEYS, (r + 1) * PEER_NKEYS)
            h_ref[rs, cs] = (w * _gelu_tanh(a_t[rs, cs])).astype(BF16)
    acc_ref[...] += jnp.dot(vt_ref[0], h_ref[...], preferred_element_type=F32)

    @pl.when(e == pl.num_programs(1) - 1)
    def _():
        y = x_ref[...] + ga_ref[0] * acc_ref[...].T
        if final:
            ms = jnp.mean(y * y, axis=-1, keepdims=True)
            y = y * lax.rsqrt(ms + RMS_EPS) * fg_ref[...]
        out_ref[...] = y


def _experts(ht, e1, th, e2, u_bf, vt_bf, x, ga, fg, per_row, rows_per_batch, tm, final):
    m = x.shape[0]
    nt = m // tm
    te = PEER_TILE
    ne = PEER_EXPERTS // te
    tpb = max(rows_per_batch // tm, 1)
    hk = PEER_HEADS * PEER_NKEYS
    sel = pl.BlockSpec((1, hk, tm + LANE_PAD), lambda i, e: (i, 0, 0))
    return pl.pallas_call(
        functools.partial(_experts_kernel, tm=tm, final=final),
        grid=(nt, ne),
        in_specs=[pl.BlockSpec((1, D, tm), lambda i, e: (i, 0, 0)), sel, sel, sel,
                  pl.BlockSpec((te, D), lambda i, e: (e, 0)),
                  pl.BlockSpec((1, D, te), lambda i, e: (e, 0, 0)),
                  pl.BlockSpec((tm, D), lambda i, e: (i, 0)),
                  _mod_spec(per_row, tm, tpb, 0),
                  pl.BlockSpec((1, D), lambda i, e: (0, 0))],
        out_specs=pl.BlockSpec((tm, D), lambda i, e: (i, 0)),
        out_shape=jax.ShapeDtypeStruct((m, D), F32),
        scratch_shapes=[pltpu.VMEM((D, tm), F32), pltpu.VMEM((te, tm), BF16)],
        compiler_params=_cparams(("arbitrary", "arbitrary")),
        name="peer_experts",
    )(ht, e1, th, e2, u_bf, vt_bf, x, ga, fg)


def _group_rows(per_row, vec, rows_per_batch, tm):
    if per_row:
        rows = jnp.repeat(vec, rows_per_batch, axis=0)
        return rows.reshape(rows.shape[0] // tm, tm, D)
    return vec[:, None, :]


def _trunk_layer(x, mods, lw, per_row, b, t, ret_st0, pool_prev, bufs, layer, pos, final, final_g):
    sh1, sc1, ga1, sh2, sc2, ga2 = mods
    m = b * t
    gm = lambda v, tm: _group_rows(per_row, v, t, tm)
    z = _inproj(x, lw["g_mix"], gm(sh1, 512), gm(sc1, 512), lw["w_in"], per_row, t)
    z3 = z.reshape(b, t, IN_WIDTH)

    cos2, sin2 = _rope_tables(pos)
    chunk = RET_CHUNK if t >= RET_CHUNK else t
    yr, new_ret = _retention(z3, ret_st0, cos2, sin2, lw["gn_g"], lw["gn_b"], chunk)

    if per_row:
        p = _pool(z3, pool_prev, t, PAST_LEN, False)
        att_parts = _datt_sample(z3, bufs, layer)
    else:
        p = _pool(z3, None, 512, 0, True)
        att_parts = [_datt_prompt(z3, gi, dil) for gi, (win, dil) in enumerate(ATT_PATTERNS)]

    x1 = _mix(x, gm(ga1, 256), z, yr.reshape(m, 512), p.reshape(m, 512), att_parts,
              lw["w_ret_out"], lw["pool_w"], lw["pool_scale"], lw["w_att_out"], lw["w_out"], per_row, t)

    tm = 512
    ht, e1, th, e2 = _route(x1, lw["g_ffn"], gm(sh2, tm), gm(sc2, tm), lw, per_row, t, tm)
    x2 = _experts(ht, e1, th, e2, lw["u"], lw["v_t"], x1, gm(ga2, tm), final_g, per_row, t, tm, final)

    new_kv = []
    for gi, (win, dil) in enumerate(ATT_PATTERNS):
        keep = t if per_row else min(win, t)
        kk = z3[:, t - keep:, (COL_KA + gi) * CB:(COL_KA + gi + 1) * CB].reshape(b, keep, ATT_HEADS, ATT_DH)
        vv = z3[:, t - keep:, (COL_VA + gi) * CB:(COL_VA + gi + 1) * CB].reshape(b, keep, ATT_HEADS, ATT_DH)
        new_kv.append(jnp.stack([kk, vv], axis=2))
    up = z3[:, :, COL_UP * CB:COL_UP * CB + 512]
    if per_row:
        new_pool = jnp.concatenate([pool_prev[:, 1:], up], axis=1)[:, -(POOL_HALO - 1):]
    else:
        new_pool = up[:, t - (POOL_HALO - 1):]
    return x2, (new_ret, new_pool, new_kv)


def kernel(x_prompt, x_sample, state_ret, state_pool, cache_kv_w128, cache_kv_w512, cache_kv_w2048,
           c_prompt, c_sample, ada_w, ada_b, norm_mix_g, norm_ffn_g, w_in, ret_gn_g, ret_gn_b,
           w_ret_out, pool_w, pool_scale, w_att_out, w_out, peer_wq, peer_keys, peer_u, peer_v,
           final_norm_g):
    depth = ada_w.shape[0]
    bp, s, _ = x_prompt.shape
    bs, t, _ = x_sample.shape
    caches_t = tuple(jnp.transpose(c, (0, 1, 3, 4, 5, 2)) for c in (cache_kv_w128, cache_kv_w512, cache_kv_w2048))

    mods_all = _adaln(jnp.concatenate([c_prompt, c_sample], axis=0), ada_w, ada_b)

    xp = x_prompt.reshape(bp * s, D)
    xs = x_sample.reshape(bs * t, D)
    final_g = final_norm_g.reshape(1, D)
    zero_state = jnp.zeros((bp, RET_HEADS, RET_DK, RET_DV), F32)
    outs_p, outs_s = [], []
    for l in range(depth):
        n_gate = 3 * D
        w_in_l = jnp.concatenate([w_in[l][:, IN_WIDTH - n_gate:], w_in[l][:, :IN_WIDTH - n_gate]], axis=1)
        wq_hi, wq_lo = _split_bf16(peer_wq[l].T)
        k1_hi, k1_lo = _split_bf16(peer_keys[l][:, 0])
        k2_hi, k2_lo = _split_bf16(peer_keys[l][:, 1])
        lw = {
            "g_mix": norm_mix_g[l].reshape(1, D), "g_ffn": norm_ffn_g[l].reshape(1, D),
            "w_in": w_in_l.astype(BF16),
            "gn_g": ret_gn_g[l].reshape(1, -1), "gn_b": ret_gn_b[l].reshape(1, -1),
            "w_ret_out": w_ret_out[l].astype(BF16), "pool_w": pool_w[l].astype(BF16),
            "pool_scale": pool_scale[l].reshape(1, D), "w_att_out": w_att_out[l].astype(BF16),
            "w_out": w_out[l].astype(BF16),
            "wq_hi": wq_hi, "wq_lo": wq_lo, "k1_hi": k1_hi, "k1_lo": k1_lo, "k2_hi": k2_hi, "k2_lo": k2_lo,
            "u": peer_u[l].astype(BF16),
            "v_t": peer_v[l].reshape(-1, PEER_TILE, D).transpose(0, 2, 1).astype(BF16),
        }
        ml = mods_all[l]
        mods_p = tuple(ml[:bp, k * D:(k + 1) * D] for k in range(6))
        mods_s = tuple(ml[bp:, k * D:(k + 1) * D] for k in range(6))
        final = l == depth - 1
        xp, st_p = _trunk_layer(xp, mods_p, lw, False, bp, s, zero_state, None, None, l,
                                jnp.arange(s), final, final_g)
        pool_prev = jnp.pad(state_pool[l], ((0, 0), (1, 0), (0, 0)))
        xs, st_s = _trunk_layer(xs, mods_s, lw, True, bs, t, state_ret[l], pool_prev, caches_t, l,
                                PAST_LEN + jnp.arange(t), final, final_g)
        outs_p.append(st_p)
        outs_s.append(st_s)

    def stack(outs, pick):
        return jnp.stack([pick(o) for o in outs])

    res = [xp.reshape(bp, s, D), xs.reshape(bs, t, D),
           stack(outs_p, lambda o: o[0]), stack(outs_s, lambda o: o[0]),
           stack(outs_p, lambda o: o[1]), stack(outs_s, lambda o: o[1])]
    for gi in range(len(ATT_PATTERNS)):
        res.append(stack(outs_p, lambda o: o[2][gi]))
        res.append(stack(outs_s, lambda o: o[2][gi]))
    return tuple(res)
```

```python
import functools
import math

import jax
import jax.numpy as jnp
from jax import lax
from jax.experimental import pallas as pl
from jax.experimental.pallas import tpu as pltpu

F32 = jnp.float32
BF16 = jnp.bfloat16

D = 1024
RMS_EPS = 1e-6
GN_EPS = 1e-5
PAST_LEN = 2048

RET_HEADS = 4
RET_DK = 64
RET_DV = 128
RET_CHUNK = 128
RET_SEQS_LONG = 2
RET_SEQS_SHORT = 8
ROPE_BASE = 10000.0
RET_LOG_GAMMA = tuple(math.log(1.0 - 2.0 ** (-5.0 - h)) for h in range(RET_HEADS))

POOL_WINDOWS = (2, 4, 8, 16)
POOL_GW = 128
POOL_HALO = 16
POOL_SEQS_SHORT = 8

ATT_PATTERNS = ((128, 1), (512, 4), (2048, 16))
ATT_HEADS = 4
ATT_DH = 64
ATT_BLOCK = 128
ATT_PAIR = 2
ATT_UNROLL = 4
ATT_BACK = 128
NEG = -1e30

PEER_HEADS = 8
PEER_NKEYS = 128
PEER_EXPERTS = PEER_NKEYS * PEER_NKEYS
PEER_TOPK = 16
PEER_TILE = 8 * PEER_NKEYS
LANE_PAD = 128

IN_WIDTH = 7424
CB = 256
COL_GATE = 0
COL_QR, COL_KR, COL_VR, COL_GR, COL_UP = 12, 13, 14, 16, 18
COL_QA, COL_KA, COL_VA = 20, 23, 26
N_CB = IN_WIDTH // CB

VMEM_LIMIT = 60 * 1024 * 1024


def _cparams(sem):
    return pltpu.CompilerParams(dimension_semantics=sem, vmem_limit_bytes=VMEM_LIMIT)


def _mod_spec(per_row, tm, tiles_per_batch, grid_pos):
    if per_row:
        return pl.BlockSpec((1, tm, D), lambda *g: (g[grid_pos], 0, 0))
    return pl.BlockSpec((1, 1, D), lambda *g: (g[grid_pos] // tiles_per_batch, 0, 0))


def _rms_mod(x, g, sh, sc):
    ms = jnp.mean(x * x, axis=-1, keepdims=True)
    h = x * lax.rsqrt(ms + RMS_EPS) * g
    return h * (1.0 + sc) + sh


def _ada_kernel(c_ref, w_ref, b_ref, o_ref):
    c = c_ref[...]
    s = c * jax.nn.sigmoid(c)
    o_ref[0] = jnp.dot(s.astype(BF16), w_ref[0].astype(BF16), preferred_element_type=F32) + b_ref[0]


def _adaln(c_all, ada_w, ada_b):
    depth, _, n = ada_w.shape
    m = c_all.shape[0]
    tn = 512
    return pl.pallas_call(
        _ada_kernel,
        grid=(depth, n // tn),
        in_specs=[pl.BlockSpec((m, D), lambda l, j: (0, 0)),
                  pl.BlockSpec((1, D, tn), lambda l, j: (l, 0, j)),
                  pl.BlockSpec((1, 1, tn), lambda l, j: (l, 0, j))],
        out_specs=pl.BlockSpec((1, m, tn), lambda l, j: (l, 0, j)),
        out_shape=jax.ShapeDtypeStruct((depth, m, n), F32),
        compiler_params=_cparams(("arbitrary", "arbitrary")),
        name="adaln",
    )(c_all, ada_w, ada_b.reshape(depth, 1, n))


def _inproj_kernel(x_ref, g_ref, sh_ref, sc_ref, w_ref, o_ref):
    h = _rms_mod(x_ref[...], g_ref[...], sh_ref[0], sc_ref[0])
    o_ref[...] = jnp.dot(h.astype(BF16), w_ref[...], preferred_element_type=F32)


def _inproj(x, g, sh, sc, w_bf, per_row, rows_per_batch):
    m = x.shape[0]
    tm = 512
    tn = IN_WIDTH // 2
    tpb = max(rows_per_batch // tm, 1)
    return pl.pallas_call(
        _inproj_kernel,
        grid=(2, m // tm),
        in_specs=[pl.BlockSpec((tm, D), lambda j, i: (i, 0)),
                  pl.BlockSpec((1, D), lambda j, i: (0, 0)),
                  _mod_spec(per_row, tm, tpb, 1),
                  _mod_spec(per_row, tm, tpb, 1),
                  pl.BlockSpec((D, tn), lambda j, i: (0, j))],
        out_specs=pl.BlockSpec((tm, tn), lambda j, i: (i, j)),
        out_shape=jax.ShapeDtypeStruct((m, IN_WIDTH), F32),
        compiler_params=_cparams(("arbitrary", "arbitrary")),
        name="inproj",
    )(x, g, sh, sc, w_bf)


def _ret_kernel(q_ref, k_ref, v_ref, g_ref, cos_ref, sin_ref, st0_ref, gng_ref, gnb_ref,
                y_ref, st_ref, st_scr, *, C, nb):
    c = pl.program_id(1)

    @pl.when(c == 0)
    def _():
        st_scr[...] = st0_ref[...]

    cos = cos_ref[...]
    sin = sin_ref[...]
    lane = lax.broadcasted_iota(jnp.int32, (C, RET_HEADS * RET_DK), 1)
    first_half = (lane % RET_DK) < (RET_DK // 2)

    def rot(x):
        nl = RET_HEADS * RET_DK
        swapped = jnp.where(first_half, pltpu.roll(x, nl - RET_DK // 2, 1), pltpu.roll(x, RET_DK // 2, 1))
        return x * cos + swapped * sin

    ii = lax.broadcasted_iota(jnp.int32, (C, C), 0)
    jj = lax.broadcasted_iota(jnp.int32, (C, C), 1)
    diff = (ii - jj).astype(F32)
    causal = ii >= jj
    row = lax.broadcasted_iota(jnp.int32, (C, 1), 0).astype(F32)
    for bi in range(nb):
        q = rot(q_ref[bi])
        k = rot(k_ref[bi]) * (RET_DK ** -0.5)
        v = v_ref[bi]
        g = g_ref[bi]
        for h in range(RET_HEADS):
            lg = RET_LOG_GAMMA[h]
            qh = q[:, h * RET_DK:(h + 1) * RET_DK].astype(BF16)
            khf = k[:, h * RET_DK:(h + 1) * RET_DK]
            vh = v[:, h * RET_DV:(h + 1) * RET_DV].astype(BF16)
            st = st_scr[bi, h]
            decay = jnp.where(causal, jnp.exp(lg * jnp.maximum(diff, 0.0)), 0.0)
            s = lax.dot_general(qh, khf.astype(BF16), (((1,), (1,)), ((), ())), preferred_element_type=F32) * decay
            o = jnp.dot(s.astype(BF16), vh, preferred_element_type=F32)
            o = o + jnp.dot(qh, st.astype(BF16), preferred_element_type=F32) * jnp.exp(lg * (row + 1.0))
            kdec = (khf * jnp.exp(lg * (C - 1.0 - row))).astype(BF16)
            st_scr[bi, h] = math.exp(lg * C) * st + lax.dot_general(
                kdec, vh, (((0,), (0,)), ((), ())), preferred_element_type=F32)
            mu = jnp.mean(o, axis=-1, keepdims=True)
            var = jnp.mean(jnp.square(o - mu), axis=-1, keepdims=True)
            on = (o - mu) * lax.rsqrt(var + GN_EPS)
            on = on * gng_ref[:, h * RET_DV:(h + 1) * RET_DV] + gnb_ref[:, h * RET_DV:(h + 1) * RET_DV]
            gh = g[:, h * RET_DV:(h + 1) * RET_DV]
            y_ref[bi, :, h * RET_DV:(h + 1) * RET_DV] = gh * jax.nn.sigmoid(gh) * on

    @pl.when(c == pl.num_programs(1) - 1)
    def _():
        st_ref[...] = st_scr[...]


def _retention(z3, st0, cos2, sin2, gn_g, gn_b, C):
    b, t, _ = z3.shape
    nc = t // C
    nb = RET_SEQS_LONG if C == RET_CHUNK else RET_SEQS_SHORT
    vq = RET_HEADS * RET_DK
    vv = RET_HEADS * RET_DV
    return pl.pallas_call(
        functools.partial(_ret_kernel, C=C, nb=nb),
        grid=(b // nb, nc),
        in_specs=[pl.BlockSpec((nb, C, vq), lambda i, c: (i, c, COL_QR)),
                  pl.BlockSpec((nb, C, vq), lambda i, c: (i, c, COL_KR)),
                  pl.BlockSpec((nb, C, vv), lambda i, c: (i, c, COL_VR // 2)),
                  pl.BlockSpec((nb, C, vv), lambda i, c: (i, c, COL_GR // 2)),
                  pl.BlockSpec((C, vq), lambda i, c: (c, 0)),
                  pl.BlockSpec((C, vq), lambda i, c: (c, 0)),
                  pl.BlockSpec((nb, RET_HEADS, RET_DK, RET_DV), lambda i, c: (i, 0, 0, 0)),
                  pl.BlockSpec((1, vv), lambda i, c: (0, 0)),
                  pl.BlockSpec((1, vv), lambda i, c: (0, 0))],
        out_specs=[pl.BlockSpec((nb, C, vv), lambda i, c: (i, c, 0)),
                   pl.BlockSpec((nb, RET_HEADS, RET_DK, RET_DV), lambda i, c: (i, 0, 0, 0))],
        out_shape=[jax.ShapeDtypeStruct((b, t, vv), F32),
                   jax.ShapeDtypeStruct((b, RET_HEADS, RET_DK, RET_DV), F32)],
        scratch_shapes=[pltpu.VMEM((nb, RET_HEADS, RET_DK, RET_DV), F32)],
        compiler_params=_cparams(("arbitrary", "arbitrary")),
        name="retention",
    )(z3, z3, z3, z3, cos2, sin2, st0, gn_g, gn_b)


def _rope_tables(pos):
    half = RET_DK // 2
    inv = 1.0 / (ROPE_BASE ** jnp.linspace(0.0, 1.0, half, dtype=F32))
    ang = pos.astype(F32)[:, None] * inv[None, :]
    cos, sin = jnp.cos(ang), jnp.sin(ang)
    cos2 = jnp.tile(jnp.concatenate([cos, cos], axis=-1), (1, RET_HEADS))
    sin2 = jnp.tile(jnp.concatenate([-sin, sin], axis=-1), (1, RET_HEADS))
    return cos2, sin2


def _pool_kernel(prev_ref, cur_ref, p_ref, *, tm, pos0, zero_first, nb):
    i = pl.program_id(1)
    row = lax.broadcasted_iota(jnp.int32, (tm, 1), 0) + i * tm + pos0
    for bi in range(nb):
        cur = cur_ref[bi]
        prev = prev_ref[bi]
        if zero_first:
            prev = jnp.where(i == 0, 0.0, prev)
        ext = jnp.concatenate([prev, cur], axis=0)
        acc = ext
        sums = []
        for s in (1, 2, 4, 8):
            acc = acc + pltpu.roll(acc, s, 0)
            sums.append(acc)
        for gi, w in enumerate(POOL_WINDOWS):
            sl = slice(gi * POOL_GW, (gi + 1) * POOL_GW)
            win = sums[gi][POOL_HALO:, sl]
            cnt = jnp.minimum(row + 1, w).astype(F32)
            p_ref[bi, :, sl] = win / cnt - cur[:, sl]


def _pool(z3, prev, tm, pos0, zero_first):
    b, t, _ = z3.shape
    nt = t // tm
    width = 4 * POOL_GW
    if prev is None:
        prev_arr = z3
        nb = 1
        hb = tm // POOL_HALO
        prev_spec = pl.BlockSpec((1, POOL_HALO, width), lambda bi, i: (bi, jnp.maximum(i * hb - 1, 0), COL_UP // 2))
    else:
        prev_arr = prev
        nb = POOL_SEQS_SHORT
        prev_spec = pl.BlockSpec((nb, POOL_HALO, width), lambda bi, i: (bi, 0, 0))
    return pl.pallas_call(
        functools.partial(_pool_kernel, tm=tm, pos0=pos0, zero_first=zero_first, nb=nb),
        grid=(b // nb, nt),
        in_specs=[prev_spec, pl.BlockSpec((nb, tm, width), lambda bi, i: (bi, i, COL_UP // 2))],
        out_specs=pl.BlockSpec((nb, tm, width), lambda bi, i: (bi, i, 0)),
        out_shape=jax.ShapeDtypeStruct((b, t, width), F32),
        compiler_params=_cparams(("arbitrary", "arbitrary")),
        name="pool",
    )(prev_arr, z3)


def _datt_kernel(q_ref, kc_ref, kp_ref, vc_ref, vp_ref, o_ref, l_ref, *, dil):
    n = pl.program_id(1)
    qb = ATT_BLOCK
    ii = lax.broadcasted_iota(jnp.int32, (qb, qb), 0)
    jj = lax.broadcasted_iota(jnp.int32, (qb, qb), 1)
    mask_c = jj <= ii
    mask_p = (jj >= ii) & (n > 0)
    nt = (((1,), (1,)), ((), ()))

    def residue(r, carry):
        rows = pl.ds(r, qb, stride=dil) if dil > 1 else pl.ds(0, qb)
        q = q_ref[0, rows, :] * (ATT_DH ** -0.5)
        kc, kp, vc, vp = kc_ref[0, rows, :], kp_ref[0, rows, :], vc_ref[0, rows, :], vp_ref[0, rows, :]
        outs, lses = [], []
        for h in range(q_ref.shape[-1] // ATT_DH):
            sl = slice(h * ATT_DH, (h + 1) * ATT_DH)
            qh = q[:, sl].astype(BF16)
            s_c = jnp.where(mask_c, lax.dot_general(qh, kc[:, sl].astype(BF16), nt, preferred_element_type=F32), NEG)
            s_p = jnp.where(mask_p, lax.dot_general(qh, kp[:, sl].astype(BF16), nt, preferred_element_type=F32), NEG)
            m = jnp.maximum(jnp.max(s_c, axis=-1, keepdims=True), jnp.max(s_p, axis=-1, keepdims=True))
            p_c = jnp.exp(s_c - m)
            p_p = jnp.exp(s_p - m)
            den = jnp.sum(p_c, axis=-1, keepdims=True) + jnp.sum(p_p, axis=-1, keepdims=True)
            num = (jnp.dot(p_c.astype(BF16), vc[:, sl].astype(BF16), preferred_element_type=F32)
                   + jnp.dot(p_p.astype(BF16), vp[:, sl].astype(BF16), preferred_element_type=F32))
            outs.append(num / den)
            lses.append(jnp.broadcast_to(m + jnp.log(den), (qb, ATT_DH)))
        o_ref[0, rows, :] = jnp.concatenate(outs, axis=-1)
        l_ref[0, rows, :] = jnp.concatenate(lses, axis=-1)
        return carry

    if dil == 1:
        residue(0, 0)
    else:
        lax.fori_loop(0, dil, residue, 0, unroll=min(dil, ATT_UNROLL))


def _datt_prompt(z3, gi, dil):
    b, s, _ = z3.shape
    tb = ATT_BLOCK * dil
    w = ATT_HEADS * ATT_DH
    heads = ATT_HEADS if dil == 1 else ATT_PAIR
    wp = heads * ATT_DH
    npair = ATT_HEADS // heads

    def spec(col, prev):
        if prev:
            return pl.BlockSpec((1, tb, wp), lambda bi, n, hp: (bi, jnp.maximum(n - 1, 0), (col + gi) * npair + hp))
        return pl.BlockSpec((1, tb, wp), lambda bi, n, hp: (bi, n, (col + gi) * npair + hp))

    o, lse = pl.pallas_call(
        functools.partial(_datt_kernel, dil=dil),
        grid=(b, s // tb, npair),
        in_specs=[spec(COL_QA, False), spec(COL_KA, False), spec(COL_KA, True),
                  spec(COL_VA, False), spec(COL_VA, True)],
        out_specs=[pl.BlockSpec((1, tb, wp), lambda bi, n, hp: (bi, n, hp))] * 2,
        out_shape=[jax.ShapeDtypeStruct((b, s, w), F32)] * 2,
        compiler_params=_cparams(("arbitrary", "arbitrary", "arbitrary")),
        name=f"datt_prompt_d{dil}",
    )(z3, z3, z3, z3, z3)
    return o.reshape(b * s, w), lse.reshape(b * s, w)


def _satt_kernel(*refs, T):
    ng = len(ATT_PATTERNS)
    qkv = refs[:3 * ng]
    bufs = refs[3 * ng:4 * ng]
    outs = refs[4 * ng:]
    nt = (((1,), (1,)), ((), ()))
    for gi, (win, dil) in enumerate(ATT_PATTERNS):
        q = qkv[gi][0] * (ATT_DH ** -0.5)
        kn = qkv[ng + gi][0]
        vn = qkv[2 * ng + gi][0]
        buf = bufs[gi]
        Wb = buf.shape[-1]
        qi_b = lax.broadcasted_iota(jnp.int32, (T, Wb), 0)
        dist_b = Wb + qi_b - lax.broadcasted_iota(jnp.int32, (T, Wb), 1)
        ok_b = (dist_b % dil == 0) & (dist_b <= dil * ATT_BACK)
        dist_n = lax.broadcasted_iota(jnp.int32, (T, T), 0) - lax.broadcasted_iota(jnp.int32, (T, T), 1)
        ok_n = (dist_n >= 0) & (dist_n % dil == 0)
        o_parts, l_parts = [], []
        for h in range(ATT_HEADS):
            sl = slice(h * ATT_DH, (h + 1) * ATT_DH)
            qh = q[:, sl].astype(BF16)
            kt = buf[0, 0, h].astype(BF16)
            vt = buf[0, 1, h].astype(BF16)
            s_b = jnp.where(ok_b, jnp.dot(qh, kt, preferred_element_type=F32), NEG)
            s_n = jnp.where(ok_n, lax.dot_general(qh, kn[:, sl].astype(BF16), nt, preferred_element_type=F32), NEG)
            m = jnp.maximum(jnp.max(s_b, axis=-1, keepdims=True), jnp.max(s_n, axis=-1, keepdims=True))
            p_b = jnp.exp(s_b - m)
            p_n = jnp.exp(s_n - m)
            den = jnp.sum(p_b, axis=-1, keepdims=True) + jnp.sum(p_n, axis=-1, keepdims=True)
            num = (lax.dot_general(p_b.astype(BF16), vt, nt, preferred_element_type=F32)
                   + jnp.dot(p_n.astype(BF16), vn[:, sl].astype(BF16), preferred_element_type=F32))
            o_parts.append(num / den)
            l_parts.append(jnp.broadcast_to(m + jnp.log(den), (T, ATT_DH)))
        outs[2 * gi][0] = jnp.concatenate(o_parts, axis=-1)
        outs[2 * gi + 1][0] = jnp.concatenate(l_parts, axis=-1)


def _datt_sample(z3, bufs_t, layer):
    b, t, _ = z3.shape
    w = ATT_HEADS * ATT_DH
    ng = len(ATT_PATTERNS)
    col = lambda c: pl.BlockSpec((1, t, w), lambda i: (i, 0, c))
    in_specs = ([col(COL_QA + gi) for gi in range(ng)] + [col(COL_KA + gi) for gi in range(ng)]
                + [col(COL_VA + gi) for gi in range(ng)]
                + [pl.BlockSpec((None, 1) + bt.shape[2:], lambda i: (layer, i, 0, 0, 0, 0)) for bt in bufs_t])
    res = pl.pallas_call(
        functools.partial(_satt_kernel, T=t),
        grid=(b,),
        in_specs=in_specs,
        out_specs=[pl.BlockSpec((1, t, w), lambda i: (i, 0, 0))] * (2 * ng),
        out_shape=[jax.ShapeDtypeStruct((b, t, w), F32)] * (2 * ng),
        compiler_params=_cparams(("arbitrary",)),
        name="datt_sample",
    )(*([z3] * (3 * ng)), *bufs_t)
    return [(res[2 * gi].reshape(b * t, w), res[2 * gi + 1].reshape(b * t, w)) for gi in range(ng)]


def _mix_kernel(x_ref, ga_ref, yr_ref, p_ref, o1_ref, l1_ref, o2_ref, l2_ref, o3_ref, l3_ref,
                gr_ref, gp_ref, gatt_ref, wr_ref, wp_ref, ps_ref, wa_ref, wo_ref, out_ref):
    y_r = jnp.dot(yr_ref[...].astype(BF16), wr_ref[...], preferred_element_type=F32)
    p = p_ref[...]
    parts = []
    for gi in range(len(POOL_WINDOWS)):
        parts.append(jnp.dot(p[:, gi * POOL_GW:(gi + 1) * POOL_GW].astype(BF16), wp_ref[gi],
                             preferred_element_type=F32))
    y_p = jnp.concatenate(parts, axis=-1) * ps_ref[...]
    l1, l2, l3 = l1_ref[...], l2_ref[...], l3_ref[...]
    lm = jnp.maximum(jnp.maximum(l1, l2), l3)
    w1, w2, w3 = jnp.exp(l1 - lm), jnp.exp(l2 - lm), jnp.exp(l3 - lm)
    att = (w1 * o1_ref[...] + w2 * o2_ref[...] + w3 * o3_ref[...]) / (w1 + w2 + w3)
    y_a = jnp.dot(att.astype(BF16), wa_ref[...], preferred_element_type=F32)
    mix = (jax.nn.sigmoid(gr_ref[...]) * y_r + jax.nn.sigmoid(gp_ref[...]) * y_p
           + jax.nn.sigmoid(gatt_ref[...]) * y_a)
    out = jnp.dot(mix.astype(BF16), wo_ref[...], preferred_element_type=F32)
    out_ref[...] = x_ref[...] + ga_ref[0] * out


def _mix(x, ga, z, yr, p, att_parts, w_ret_out, pool_w, pool_scale, w_att_out, w_out, per_row, rows_per_batch):
    m = x.shape[0]
    tm = 256
    tpb = max(rows_per_batch // tm, 1)
    row = lambda width: pl.BlockSpec((tm, width), lambda i: (i, 0))
    full = lambda shape: pl.BlockSpec(shape, lambda i: (0,) * len(shape))
    gate = lambda k: pl.BlockSpec((tm, D), lambda i: (i, k))
    (o1, l1), (o2, l2), (o3, l3) = att_parts
    return pl.pallas_call(
        _mix_kernel,
        grid=(m // tm,),
        in_specs=[row(D), _mod_spec(per_row, tm, tpb, 0), row(512), row(512),
                  row(256), row(256), row(256), row(256), row(256), row(256),
                  gate(0), gate(1), gate(2),
                  full((512, D)), full((4, POOL_GW, 256)), full((1, D)), full((256, D)), full((D, D))],
        out_specs=row(D),
        out_shape=jax.ShapeDtypeStruct((m, D), F32),
        compiler_params=_cparams(("arbitrary",)),
        name="mix",
    )(x, ga, yr, p, o1, l1, o2, l2, o3, l3, z, z, z, w_ret_out, pool_w, pool_scale, w_att_out, w_out)


def _split_bf16(x):
    hi = x.astype(BF16)
    return hi, (x - hi.astype(F32)).astype(BF16)


def _dot3(a_hi, a_lo, b_hi, b_lo):
    d = functools.partial(jnp.dot, preferred_element_type=F32)
    return d(a_hi, b_hi) + (d(a_hi, b_lo) + d(a_lo, b_hi))


def _sorting_network(n):
    pairs, p = [], 1
    while p < n:
        k = p
        while k >= 1:
            for j in range(k % p, n - k, 2 * k):
                for i in range(min(k, n - j - k)):
                    if (i + j) // (2 * p) == (i + j + k) // (2 * p):
                        pairs.append((i + j, i + j + k))
            k //= 2
        p *= 2
    return tuple(pairs)


_SORT16 = _sorting_network(PEER_TOPK)


def _extract_top(s, v_scr, tm):
    nt8 = PEER_NKEYS // 8
    for c in range(tm // 128):
        cs = slice(c * 128, (c + 1) * 128)
        a = [s[8 * j:8 * (j + 1), cs] for j in range(nt8)]
        for i, j in _SORT16:
            a[i], a[j] = jnp.maximum(a[i], a[j]), jnp.minimum(a[i], a[j])
        for k in range(PEER_TOPK):
            mx = jnp.max(a[0], axis=0, keepdims=True)
            v_scr[k:k + 1, cs] = mx
            pop = a[0] == mx
            last = PEER_TOPK - 1 - k
            for j in range(last):
                a[j] = jnp.where(pop, a[j + 1], a[j])
            a[last] = jnp.where(pop, NEG, a[last])


def _route_kernel(x_ref, g_ref, sh_ref, sc_ref, wqh_ref, wql_ref, k1h_ref, k1l_ref, k2h_ref, k2l_ref,
                  ht_ref, e1_ref, th_ref, e2_ref, v1_scr, v2_scr, *, tm):
    h2 = _rms_mod(x_ref[...], g_ref[...], sh_ref[0], sc_ref[0])
    ht = h2.T
    ht_hi, ht_lo = _split_bf16(ht)
    ht_ref[0] = ht_hi
    qt = _dot3(wqh_ref[...], wql_ref[...], ht_hi, ht_lo)
    half = PEER_NKEYS // 2
    sub = lax.broadcasted_iota(jnp.int32, (8, tm), 0)
    none = 2.0
    for h in range(PEER_HEADS):
        base = h * PEER_NKEYS
        q1h, q1l = _split_bf16(qt[base:base + half])
        q2h, q2l = _split_bf16(qt[base + half:base + PEER_NKEYS])
        s1 = _dot3(k1h_ref[h], k1l_ref[h], q1h, q1l)
        s2 = _dot3(k2h_ref[h], k2l_ref[h], q2h, q2l)
        e1 = jnp.exp(s1 - jnp.max(s1, axis=0, keepdims=True))
        e2 = jnp.exp(s2 - jnp.max(s2, axis=0, keepdims=True))
        _extract_top(e1, v1_scr, tm)
        _extract_top(e2, v2_scr, tm)
        ev1 = v1_scr[...]
        ev2 = v2_scr[...]
        blocks = [ev1[0:1] * ev2, ev1[1:2] * ev2[0:8]]
        for a in range(2, 8):
            blocks.append(jnp.where(sub < PEER_TOPK // (a + 1), ev1[a:a + 1] * ev2[0:8], -1.0))
        blocks.append(ev1[8:16] * ev2[0:1])
        cand = jnp.concatenate(blocks, axis=0)
        rem = cand
        thr = None
        for _ in range(PEER_TOPK):
            thr = jnp.max(rem, axis=0, keepdims=True)
            rem = jnp.where(rem == thr, -1.0, rem)
        sel = cand >= thr
        z = jnp.sum(jnp.where(sel, cand, 0.0), axis=0, keepdims=True)
        rz = 1.0 / z
        low = jnp.where(sel, jnp.concatenate([ev2] + [ev2[0:8]] * 7 + [jnp.broadcast_to(ev2[0:1], (8, tm))], axis=0),
                        none)
        th_a = [jnp.min(low[0:16], axis=0, keepdims=True)]
        for a in range(1, 8):
            th_a.append(jnp.min(low[8 + 8 * a:16 + 8 * a], axis=0, keepdims=True))
        for a in range(8, 16):
            th_a.append(low[72 + a - 8:73 + a - 8])
        thmap = jnp.full((PEER_NKEYS, tm), none, F32)
        for a in range(PEER_TOPK):
            thmap = jnp.where(e1 == ev1[a:a + 1], th_a[a], thmap)
        rows = slice(base, base + PEER_NKEYS)
        for ref, val in ((e1_ref, e1 * rz), (th_ref, thmap), (e2_ref, e2)):
            ref[0, rows, 0:tm] = val
            ref[0, rows, tm:] = jnp.zeros((PEER_NKEYS, LANE_PAD), F32)


def _route(x, g, sh, sc, lw, per_row, rows_per_batch, tm):
    m = x.shape[0]
    nt = m // tm
    tpb = max(rows_per_batch // tm, 1)
    hk = PEER_HEADS * PEER_NKEYS
    full = lambda a: pl.BlockSpec(a.shape, lambda i: (0,) * a.ndim)
    tile = lambda r: pl.BlockSpec((1, r, tm), lambda i: (i, 0, 0))
    ws = [lw["wq_hi"], lw["wq_lo"], lw["k1_hi"], lw["k1_lo"], lw["k2_hi"], lw["k2_lo"]]
    return pl.pallas_call(
        functools.partial(_route_kernel, tm=tm),
        grid=(nt,),
        in_specs=[pl.BlockSpec((tm, D), lambda i: (i, 0)), full(g),
                  _mod_spec(per_row, tm, tpb, 0), _mod_spec(per_row, tm, tpb, 0)] + [full(a) for a in ws],
        out_specs=[tile(D)] + [pl.BlockSpec((1, hk, tm + LANE_PAD), lambda i: (i, 0, 0))] * 3,
        out_shape=[jax.ShapeDtypeStruct((nt, D, tm), BF16)]
        + [jax.ShapeDtypeStruct((nt, hk, tm + LANE_PAD), F32)] * 3,
        scratch_shapes=[pltpu.VMEM((PEER_TOPK, tm), F32), pltpu.VMEM((PEER_TOPK, tm), F32)],
        compiler_params=_cparams(("arbitrary",)),
        name="peer_route",
    )(x, g, sh, sc, *ws)


def _gelu_tanh(x):
    return 0.5 * x * (1.0 + jnp.tanh(0.7978845608028654 * (x + 0.044715 * (x * x * x))))


def _experts_kernel(ht_ref, e1_ref, th_ref, e2_ref, u_ref, vt_ref, x_ref, ga_ref, fg_ref,
                    out_ref, acc_ref, h_ref, *, tm, final):
    e = pl.program_id(1)
    te = PEER_TILE
    lc = 128
    n_i1 = te // PEER_NKEYS

    @pl.when(e == 0)
    def _():
        acc_ref[...] = jnp.zeros_like(acc_ref)

    a_t = jnp.dot(u_ref[...], ht_ref[0], preferred_element_type=F32)
    for r in range(n_i1):
        for c in range(tm // lc):
            cs = slice(c * lc, (c + 1) * lc)
            w = None
            for h in range(PEER_HEADS):
                start = pl.multiple_of(h * PEER_NKEYS + e * n_i1, n_i1)
                e1row = e1_ref[0, pl.ds(start, n_i1), cs][r:r + 1]
                throw = th_ref[0, pl.ds(start, n_i1), cs][r:r + 1]
                e2 = e2_ref[0, h * PEER_NKEYS:(h + 1) * PEER_NKEYS, cs]
                term = jnp.where(e2 >= throw, e1row * e2, 0.0)
                w = term if w is None else w + term
            rs = slice(r * PEER_NKEYS, (r + 1) * PEER_NKEYS)
            h_ref[rs, cs] = (w * _gelu_tanh(a_t[rs, cs])).astype(BF16)
    acc_ref[...] += jnp.dot(vt_ref[0], h_ref[...], preferred_element_type=F32)

    @pl.when(e == pl.num_programs(1) - 1)
    def _():
        y = x_ref[...] + ga_ref[0] * acc_ref[...].T
        if final:
            ms = jnp.mean(y * y, axis=-1, keepdims=True)
            y = y * lax.rsqrt(ms + RMS_EPS) * fg_ref[...]
        out_ref[...] = y


def _experts(ht, e1, th, e2, u_bf, vt_bf, x, ga, fg, per_row, rows_per_batch, tm, final):
    m = x.shape[0]
    nt = m // tm
    te = PEER_TILE
    ne = PEER_EXPERTS // te
    tpb = max(rows_per_batch // tm, 1)
    hk = PEER_HEADS * PEER_NKEYS
    sel = pl.BlockSpec((1, hk, tm + LANE_PAD), lambda i, e: (i, 0, 0))
    return pl.pallas_call(
        functools.partial(_experts_kernel, tm=tm, final=final),
        grid=(nt, ne),
        in_specs=[pl.BlockSpec((1, D, tm), lambda i, e: (i, 0, 0)), sel, sel, sel,
                  pl.BlockSpec((te, D), lambda i, e: (e, 0)),
                  pl.BlockSpec((1, D, te), lambda i, e: (e, 0, 0)),
                  pl.BlockSpec((tm, D), lambda i, e: (i, 0)),
                  _mod_spec(per_row, tm, tpb, 0),
                  pl.BlockSpec((1, D), lambda i, e: (0, 0))],
        out_specs=pl.BlockSpec((tm, D), lambda i, e: (i, 0)),
        out_shape=jax.ShapeDtypeStruct((m, D), F32),
        scratch_shapes=[pltpu.VMEM((D, tm), F32), pltpu.VMEM((te, tm), BF16)],
        compiler_params=_cparams(("arbitrary", "arbitrary")),
        name="peer_experts",
    )(ht, e1, th, e2, u_bf, vt_bf, x, ga, fg)


def _group_rows(per_row, vec, rows_per_batch, tm):
    if per_row:
        rows = jnp.repeat(vec, rows_per_batch, axis=0)
        return rows.reshape(rows.shape[0] // tm, tm, D)
    return vec[:, None, :]


def _trunk_layer(x, mods, lw, per_row, b, t, ret_st0, pool_prev, bufs, layer, pos, final, final_g):
    sh1, sc1, ga1, sh2, sc2, ga2 = mods
    m = b * t
    gm = lambda v, tm: _group_rows(per_row, v, t, tm)
    z = _inproj(x, lw["g_mix"], gm(sh1, 512), gm(sc1, 512), lw["w_in"], per_row, t)
    z3 = z.reshape(b, t, IN_WIDTH)

    cos2, sin2 = _rope_tables(pos)
    chunk = RET_CHUNK if t >= RET_CHUNK else t
    yr, new_ret = _retention(z3, ret_st0, cos2, sin2, lw["gn_g"], lw["gn_b"], chunk)

    if per_row:
        p = _pool(z3, pool_prev, t, PAST_LEN, False)
        att_parts = _datt_sample(z3, bufs, layer)
    else:
        p = _pool(z3, None, 512, 0, True)
        att_parts = [_datt_prompt(z3, gi, dil) for gi, (win, dil) in enumerate(ATT_PATTERNS)]

    x1 = _mix(x, gm(ga1, 256), z, yr.reshape(m, 512), p.reshape(m, 512), att_parts,
              lw["w_ret_out"], lw["pool_w"], lw["pool_scale"], lw["w_att_out"], lw["w_out"], per_row, t)

    tm = 512
    ht, e1, th, e2 = _route(x1, lw["g_ffn"], gm(sh2, tm), gm(sc2, tm), lw, per_row, t, tm)
    x2 = _experts(ht, e1, th, e2, lw["u"], lw["v_t"], x1, gm(ga2, tm), final_g, per_row, t, tm, final)

    new_kv = []
    for gi, (win, dil) in enumerate(ATT_PATTERNS):
        keep = t if per_row else min(win, t)
        kk = z3[:, t - keep:, (COL_KA + gi) * CB:(COL_KA + gi + 1) * CB].reshape(b, keep, ATT_HEADS, ATT_DH)
        vv = z3[:, t - keep:, (COL_VA + gi) * CB:(COL_VA + gi + 1) * CB].reshape(b, keep, ATT_HEADS, ATT_DH)
        new_kv.append(jnp.stack([kk, vv], axis=2))
    up = z3[:, :, COL_UP * CB:COL_UP * CB + 512]
    if per_row:
        new_pool = jnp.concatenate([pool_prev[:, 1:], up], axis=1)[:, -(POOL_HALO - 1):]
    else:
        new_pool = up[:, t - (POOL_HALO - 1):]
    return x2, (new_ret, new_pool, new_kv)


def kernel(x_prompt, x_sample, state_ret, state_pool, cache_kv_w128, cache_kv_w512, cache_kv_w2048,
           c_prompt, c_sample, ada_w, ada_b, norm_mix_g, norm_ffn_g, w_in, ret_gn_g, ret_gn_b,
           w_ret_out, pool_w, pool_scale, w_att_out, w_out, peer_wq, peer_keys, peer_u, peer_v,
           final_norm_g):
    depth = ada_w.shape[0]
    bp, s, _ = x_prompt.shape
    bs, t, _ = x_sample.shape
    caches_t = tuple(jnp.transpose(c, (0, 1, 3, 4, 5, 2)) for c in (cache_kv_w128, cache_kv_w512, cache_kv_w2048))

    mods_all = _adaln(jnp.concatenate([c_prompt, c_sample], axis=0), ada_w, ada_b)

    xp = x_prompt.reshape(bp * s, D)
    xs = x_sample.reshape(bs * t, D)
    final_g = final_norm_g.reshape(1, D)
    zero_state = jnp.zeros((bp, RET_HEADS, RET_DK, RET_DV), F32)
    outs_p, outs_s = [], []
    for l in range(depth):
        n_gate = 3 * D
        w_in_l = jnp.concatenate([w_in[l][:, IN_WIDTH - n_gate:], w_in[l][:, :IN_WIDTH - n_gate]], axis=1)
        wq_hi, wq_lo = _split_bf16(peer_wq[l].T)
        k1_hi, k1_lo = _split_bf16(peer_keys[l][:, 0])
        k2_hi, k2_lo = _split_bf16(peer_keys[l][:, 1])
        lw = {
            "g_mix": norm_mix_g[l].reshape(1, D), "g_ffn": norm_ffn_g[l].reshape(1, D),
            "w_in": w_in_l.astype(BF16),
            "gn_g": ret_gn_g[l].reshape(1, -1), "gn_b": ret_gn_b[l].reshape(1, -1),
            "w_ret_out": w_ret_out[l].astype(BF16), "pool_w": pool_w[l].astype(BF16),
            "pool_scale": pool_scale[l].reshape(1, D), "w_att_out": w_att_out[l].astype(BF16),
            "w_out": w_out[l].astype(BF16),
            "wq_hi": wq_hi, "wq_lo": wq_lo, "k1_hi": k1_hi, "k1_lo": k1_lo, "k2_hi": k2_hi, "k2_lo": k2_lo,
            "u": peer_u[l].astype(BF16),
            "v_t": peer_v[l].reshape(-1, PEER_TILE, D).transpose(0, 2, 1).astype(BF16),
        }
        ml = mods_all[l]
        mods_p = tuple(ml[:bp, k * D:(k + 1) * D] for k in range(6))
        mods_s = tuple(ml[bp:, k * D:(k + 1) * D] for k in range(6))
        final = l == depth - 1
        xp, st_p = _trunk_layer(xp, mods_p, lw, False, bp, s, zero_state, None, None, l,
                                jnp.arange(s), final, final_g)
        pool_prev = jnp.pad(state_pool[l], ((0, 0), (1, 0), (0, 0)))
        xs, st_s = _trunk_layer(xs, mods_s, lw, True, bs, t, state_ret[l], pool_prev, caches_t, l,
                                PAST_LEN + jnp.arange(t), final, final_g)
        outs_p.append(st_p)
        outs_s.append(st_s)

    def stack(outs, pick):
        return jnp.stack([pick(o) for o in outs])

    res = [xp.reshape(bp, s, D), xs.reshape(bs, t, D),
           stack(outs_p, lambda o: o[0]), stack(outs_s, lambda o: o[0]),
           stack(outs_p, lambda o: o[1]), stack(outs_s, lambda o: o[1])]
    for gi in range(len(ATT_PATTERNS)):
        res.append(stack(outs_p, lambda o: o[2][gi]))
        res.append(stack(outs_s, lambda o: o[2][gi]))
    return tuple(res)
```
